```python
import jax, jax.numpy as jnp
from jax import lax
import numpy as np

D_MODEL = 2048
BATCH = 2
SEQ = 8192
DEPTH = 1

PLE_DIM = 256
EPS = 1e-6

N_HEADS_MLA = 16
Q_LORA = 512
KV_LORA = 512
QK_NOPE = 128
QK_ROPE = 64
V_DIM = 128
QK_DIM = QK_NOPE + QK_ROPE
ROPE_THETA = 10000.0
Q_BLOCK = 128

SSM_EXPAND = 2
D_INNER = SSM_EXPAND * D_MODEL
SSM_HEADDIM = 64
N_HEADS_SSM = D_INNER // SSM_HEADDIM
SSM_GROUPS = 8
HEADS_PER_GROUP = N_HEADS_SSM // SSM_GROUPS
D_STATE = 128
CONV_WIDTH = 4
CHUNK = 256
CONV_DIM = D_INNER + 2 * SSM_GROUPS * D_STATE

D_FF = ((8 * D_MODEL // 3 + 255) // 256) * 256

IN_SPLITS = (Q_LORA, KV_LORA, QK_ROPE, D_INNER, CONV_DIM, N_HEADS_SSM, D_MODEL, D_MODEL)
D_IN_PROJ = Q_LORA + KV_LORA + QK_ROPE + D_INNER + CONV_DIM + N_HEADS_SSM + 2 * D_MODEL

kernel_name = "hybrid_mla_ssd_gated_block"


def rms_norm(x, w):
    xf = x.astype(jnp.float32)
    y = xf * lax.rsqrt(jnp.mean(xf * xf, axis=-1, keepdims=True) + EPS)
    return (y * w.astype(jnp.float32)).astype(x.dtype)


def split_cols(t, sizes):
    outs, off = [], 0
    for s in sizes:
        outs.append(t[..., off:off + s])
        off += s
    return outs


def rotary_tables(positions, dim):
    inv_freq = ROPE_THETA ** (-jnp.arange(0, dim, 2, dtype=jnp.float32) / dim)
    ang = positions.astype(jnp.float32)[..., None] * inv_freq
    return jnp.cos(ang)[:, :, None, :], jnp.sin(ang)[:, :, None, :]


def apply_rope(x, cos, sin):
    xf = x.astype(jnp.float32)
    x1, x2 = jnp.split(xf, 2, axis=-1)
    out = jnp.concatenate([x1 * cos - x2 * sin, x2 * cos + x1 * sin], axis=-1)
    return out.astype(x.dtype)


def causal_block_attention(q, k, v):
    b, s, h, dk = q.shape
    nblk = s // Q_BLOCK
    scale = dk ** -0.5
    qb = jnp.moveaxis(q.reshape(b, nblk, Q_BLOCK, h, dk), 1, 0)
    key_idx = jnp.arange(s)

    def one_block(args):
        q_blk, blk = args
        sc = jnp.einsum('bqhd,bkhd->bhqk', q_blk, k, preferred_element_type=jnp.float32) * scale
        q_idx = blk * Q_BLOCK + jnp.arange(Q_BLOCK)
        mask = key_idx[None, :] <= q_idx[:, None]
        sc = jnp.where(mask[None, None], sc, -jnp.inf)
        prob = jax.nn.softmax(sc, axis=-1).astype(v.dtype)
        return jnp.einsum('bhqk,bkhd->bqhd', prob, v)

    o = lax.map(one_block, (qb, jnp.arange(nblk)))
    return jnp.moveaxis(o, 0, 1).reshape(b, s, h * v.shape[-1])


def mla_branch(c_q, c_kv, k_r, cos, sin, q_norm, w_uq, kv_norm, w_ukv):
    b, s, _ = c_q.shape
    q = (rms_norm(c_q, q_norm) @ w_uq).reshape(b, s, N_HEADS_MLA, QK_DIM)
    q_nope, q_pe = q[..., :QK_NOPE], apply_rope(q[..., QK_NOPE:], cos, sin)
    kv = (rms_norm(c_kv, kv_norm) @ w_ukv).reshape(b, s, N_HEADS_MLA, QK_NOPE + V_DIM)
    k_nope, v = kv[..., :QK_NOPE], kv[..., QK_NOPE:]
    k_pe = apply_rope(k_r[:, :, None, :], cos, sin)
    k = jnp.concatenate([k_nope, jnp.broadcast_to(k_pe, (b, s, N_HEADS_MLA, QK_ROPE))], axis=-1)
    q = jnp.concatenate([q_nope, q_pe], axis=-1)
    return causal_block_attention(q, k, v)


def causal_depthwise_conv(t, w, bias):
    out = lax.conv_general_dilated(
        t, w[:, None, :].astype(t.dtype), window_strides=(1,),
        padding=[(CONV_WIDTH - 1, 0)], dimension_numbers=('NWC', 'WIO', 'NWC'),
        feature_group_count=t.shape[-1])
    return out + bias


def ssd_chunked(x, dt, a, bm, cm):
    b, s, g, r, p = x.shape
    n = bm.shape[-1]
    nc = -(-s // CHUNK)
    pad = nc * CHUNK - s
    padw = lambda t: jnp.pad(t, [(0, 0), (0, pad)] + [(0, 0)] * (t.ndim - 2))
    xdt = padw(x * dt[..., None])
    da = padw(dt * a)
    bm, cm = padw(bm), padw(cm)
    chunks = lambda t: jnp.moveaxis(t.reshape((b, nc, CHUNK) + t.shape[2:]), 1, 0)
    causal = jnp.tril(jnp.ones((CHUNK, CHUNK), dtype=bool))[None, :, :, None, None]

    def step(state, inp):
        xc, ac, bc, cc = inp
        cum = jnp.cumsum(ac, axis=1)
        seg = cum[:, :, None] - cum[:, None, :]
        decay = jnp.exp(jnp.where(causal, seg, -jnp.inf))
        cb = jnp.einsum('bign,bjgn->bijg', cc, bc)
        y_diag = jnp.einsum('bijgr,bjgrp->bigrp', cb[..., None] * decay, xc)
        y_off = jnp.einsum('bign,bgrpn->bigrp', cc, state) * jnp.exp(cum)[..., None]
        last = cum[:, -1]
        w_end = jnp.exp(last[:, None] - cum)
        state = state * jnp.exp(last)[..., None, None] + jnp.einsum(
            'bjgn,bjgrp->bgrpn', bc, xc * w_end[..., None])
        return state, y_diag + y_off

    state0 = jnp.zeros((b, g, r, p, n), jnp.float32)
    _, ys = lax.scan(step, state0, (chunks(xdt), chunks(da), chunks(bm), chunks(cm)))
    ys = jnp.moveaxis(ys, 0, 1).reshape(b, nc * CHUNK, g, r, p)
    return ys[:, :s]


def mamba2_branch(z, xbc, dt_raw, conv_w, conv_b, dt_bias, a_log, d_skip, ssm_norm):
    b, s, _ = z.shape
    xbc = jax.nn.silu(causal_depthwise_conv(xbc, conv_w, conv_b))
    xs, bm, cm = split_cols(xbc, (D_INNER, SSM_GROUPS * D_STATE, SSM_GROUPS * D_STATE))
    xh = xs.reshape(b, s, SSM_GROUPS, HEADS_PER_GROUP, SSM_HEADDIM).astype(jnp.float32)
    dt = jax.nn.softplus(dt_raw.astype(jnp.float32) + dt_bias.astype(jnp.float32))
    dt = dt.reshape(b, s, SSM_GROUPS, HEADS_PER_GROUP)
    a = -jnp.exp(a_log.astype(jnp.float32)).reshape(SSM_GROUPS, HEADS_PER_GROUP)
    y = ssd_chunked(xh, dt, a,
                    bm.reshape(b, s, SSM_GROUPS, D_STATE).astype(jnp.float32),
                    cm.reshape(b, s, SSM_GROUPS, D_STATE).astype(jnp.float32))
    y = y + d_skip.astype(jnp.float32).reshape(SSM_GROUPS, HEADS_PER_GROUP)[..., None] * xh
    y = y.reshape(b, s, D_INNER).astype(z.dtype) * jax.nn.silu(z)
    y = rms_norm(y.reshape(b, s, SSM_GROUPS, D_INNER // SSM_GROUPS),
                 ssm_norm.reshape(SSM_GROUPS, D_INNER // SSM_GROUPS))
    return y.reshape(b, s, D_INNER)


def setup_inputs(seed: int = 0) -> dict:
    key = jax.random.key(seed)
    ks = jax.random.split(key, 32)
    f32 = jnp.float32
    nrm = lambda k, shape, fan_in: jax.random.normal(k, shape, f32) * fan_in ** -0.5
    gain = lambda k, dim: 1.0 + 0.02 * jax.random.normal(k, (DEPTH, dim), f32)
    dt0 = jnp.exp(jax.random.uniform(ks[12], (DEPTH, N_HEADS_SSM), f32)
                  * (np.log(0.1) - np.log(0.001)) + np.log(0.001))
    return {
        "x": jax.random.normal(ks[0], (BATCH, SEQ, D_MODEL), f32),
        "p": jax.random.normal(ks[1], (DEPTH, BATCH, SEQ, PLE_DIM), f32),
        "positions": jnp.broadcast_to(jnp.arange(SEQ, dtype=jnp.int32), (BATCH, SEQ)),
        "mix_norm_pre": gain(ks[2], D_MODEL),
        "mix_norm_post": gain(ks[3], D_MODEL),
        "w_in": nrm(ks[4], (DEPTH, D_MODEL, D_IN_PROJ), D_MODEL),
        "q_norm": gain(ks[5], Q_LORA),
        "w_uq": nrm(ks[6], (DEPTH, Q_LORA, N_HEADS_MLA * QK_DIM), Q_LORA),
        "kv_norm": gain(ks[7], KV_LORA),
        "w_ukv": nrm(ks[8], (DEPTH, KV_LORA, N_HEADS_MLA * (QK_NOPE + V_DIM)), KV_LORA),
        "conv_w": nrm(ks[9], (DEPTH, CONV_WIDTH, CONV_DIM), CONV_WIDTH),
        "conv_b": 0.01 * jax.random.normal(ks[10], (DEPTH, CONV_DIM), f32),
        "dt_bias": dt0 + jnp.log(-jnp.expm1(-dt0)),
        "a_log": jnp.log(jax.random.uniform(ks[11], (DEPTH, N_HEADS_SSM), f32, 1.0, 16.0)),
        "d_skip": 1.0 + 0.02 * jax.random.normal(ks[13], (DEPTH, N_HEADS_SSM), f32),
        "ssm_norm": gain(ks[14], D_INNER),
        "w_attn_o": nrm(ks[15], (DEPTH, N_HEADS_MLA * V_DIM, D_MODEL), N_HEADS_MLA * V_DIM),
        "w_ssm_o": nrm(ks[16], (DEPTH, D_INNER, D_MODEL), D_INNER),
        "w_out": nrm(ks[17], (DEPTH, D_MODEL, D_MODEL), D_MODEL),
        "ffn_norm_pre": gain(ks[18], D_MODEL),
        "ffn_norm_post": gain(ks[19], D_MODEL),
        "w_gate": nrm(ks[20], (DEPTH, D_MODEL, D_FF), D_MODEL),
        "w_up": nrm(ks[21], (DEPTH, D_MODEL, D_FF), D_MODEL),
        "w_down": nrm(ks[22], (DEPTH, D_FF, D_MODEL), D_FF),
        "ple_norm_pre": gain(ks[23], D_MODEL),
        "ple_norm_post": gain(ks[24], D_MODEL),
        "w_ple_gate": nrm(ks[25], (DEPTH, D_MODEL, D_MODEL), D_MODEL),
        "w_ple": nrm(ks[26], (DEPTH, PLE_DIM, D_MODEL), PLE_DIM),
    }


def reference(x, p, positions, mix_norm_pre, mix_norm_post, w_in, q_norm, w_uq, kv_norm, w_ukv,
              conv_w, conv_b, dt_bias, a_log, d_skip, ssm_norm, w_attn_o, w_ssm_o, w_out,
              ffn_norm_pre, ffn_norm_post, w_gate, w_up, w_down,
              ple_norm_pre, ple_norm_post, w_ple_gate, w_ple):
    cos, sin = rotary_tables(positions, QK_ROPE)
    h = x
    for i in range(DEPTH):
        u = rms_norm(h, mix_norm_pre[i])
        proj = u @ w_in[i]
        c_q, c_kv, k_r, z, xbc, dt_raw, g_attn, g_ssm = split_cols(proj, IN_SPLITS)
        attn = mla_branch(c_q, c_kv, k_r, cos, sin, q_norm[i], w_uq[i], kv_norm[i], w_ukv[i])
        ssm = mamba2_branch(z, xbc, dt_raw, conv_w[i], conv_b[i], dt_bias[i], a_log[i],
                            d_skip[i], ssm_norm[i])
        mixed = jax.nn.sigmoid(g_attn) * (attn @ w_attn_o[i]) + jax.nn.sigmoid(g_ssm) * (ssm @ w_ssm_o[i])
        h = h + rms_norm(mixed @ w_out[i], mix_norm_post[i])
        f = rms_norm(h, ffn_norm_pre[i])
        f = (jax.nn.silu(f @ w_gate[i]) * (f @ w_up[i])) @ w_down[i]
        h = h + rms_norm(f, ffn_norm_post[i])
        gate = jax.nn.sigmoid(rms_norm(h, ple_norm_pre[i]) @ w_ple_gate[i])
        e = (p[i].astype(h.dtype) @ w_ple[i]) * gate
        h = h + rms_norm(e, ple_norm_post[i])
    return h
```

```python
import functools

import jax
import jax.numpy as jnp
import numpy as np
from jax import lax
from jax.experimental import pallas as pl
from jax.experimental.pallas import tpu as pltpu

F32 = jnp.float32
BF16 = jnp.bfloat16

EPS = 1e-6
ROPE_THETA = 10000.0

N_HEADS_MLA = 16
Q_LORA = 512
KV_LORA = 512
QK_NOPE = 128
QK_ROPE = 64
V_DIM = 128
QK_DIM = QK_NOPE + QK_ROPE
QK_PAD = 256

SSM_HEADDIM = 64
SSM_GROUPS = 8
HEADS_PER_GROUP = 8
GROUP_WIDTH = HEADS_PER_GROUP * SSM_HEADDIM
D_STATE = 128
CONV_WIDTH = 4
CHUNK = 256
CONV_TAIL = 8

VMEM_LIMIT = 56 * 1024 * 1024


def _params(*sem):
    return pltpu.CompilerParams(dimension_semantics=sem, vmem_limit_bytes=VMEM_LIMIT)


def _rms(x, w):
    return x * lax.rsqrt(jnp.mean(x * x, axis=-1, keepdims=True) + EPS) * w


def _sigmoid(x):
    return 1.0 / (1.0 + jnp.exp(-x))


def _dot(a, b):
    return jnp.dot(a, b, preferred_element_type=F32)


def _dot_nt(a, b):
    return lax.dot_general(a, b, (((1,), (1,)), ((), ())), preferred_element_type=F32)


def _inproj_kernel(x_ref, nw_ref, w_ref, wkr_ref, wdt_ref, o_ref, okr_ref, odt_ref, u_scr):
    @pl.when(pl.program_id(1) == 0)
    def _():
        u = _rms(x_ref[...], nw_ref[...]).astype(BF16)
        u_scr[...] = u
        okr_ref[...] = _dot(u, wkr_ref[...])
        odt_ref[...] = _dot_nt(wdt_ref[...], u)

    o_ref[...] = _dot(u_scr[...], w_ref[...]).astype(BF16)


def _inproj(x2, norm_w, w_main, w_kr, w_dt_t, tm, tn):
    t, d = x2.shape
    n_main = w_main.shape[1]
    n_dt = w_dt_t.shape[0]
    return pl.pallas_call(
        _inproj_kernel,
        grid=(t // tm, n_main // tn),
        in_specs=[
            pl.BlockSpec((tm, d), lambda m, n: (m, 0)),
            pl.BlockSpec((1, d), lambda m, n: (0, 0)),
            pl.BlockSpec((d, tn), lambda m, n: (0, n)),
            pl.BlockSpec((d, 128), lambda m, n: (0, 0)),
            pl.BlockSpec((n_dt, d), lambda m, n: (0, 0)),
        ],
        out_specs=[
            pl.BlockSpec((tm, tn), lambda m, n: (m, n)),
            pl.BlockSpec((tm, 128), lambda m, n: (m, 0)),
            pl.BlockSpec((n_dt, tm), lambda m, n: (0, m)),
        ],
        out_shape=[
            jax.ShapeDtypeStruct((t, n_main), BF16),
            jax.ShapeDtypeStruct((t, 128), F32),
            jax.ShapeDtypeStruct((n_dt, t), F32),
        ],
        scratch_shapes=[pltpu.VMEM((tm, d), BF16)],
        compiler_params=_params("parallel", "arbitrary"),
        name="inproj",
    )(x2, norm_w, w_main, w_kr, w_dt_t)


def _mla_prep_kernel(cq_ref, ckv_ref, kr_ref, pos_ref, invf_ref, qn_ref, kvn_ref, wq_ref, wkv_ref,
                     q_ref, k_ref, v_ref, cqn_scr, ckvn_scr, tab_scr, kpe_scr):
    lane = lax.broadcasted_iota(jnp.int32, tab_scr.shape, 1)
    low_half = lane < QK_ROPE

    def rope(pair):
        t = pair * tab_scr[...]
        return jnp.where(low_half, t + pltpu.roll(t, QK_ROPE, axis=1), 0.0)

    @pl.when(pl.program_id(1) == 0)
    def _():
        cqn_scr[...] = _rms(cq_ref[...].astype(F32), qn_ref[...]).astype(BF16)
        ckvn_scr[...] = _rms(ckv_ref[...].astype(F32), kvn_ref[...]).astype(BF16)
        ang = pos_ref[...].astype(F32) * invf_ref[...]
        sin = jnp.sin(ang)
        neg = (lane >= QK_ROPE) & (lane < QK_ROPE + QK_ROPE // 2)
        tab_scr[...] = jnp.where(low_half, jnp.cos(ang), jnp.where(neg, -sin, sin))
        kpe_scr[...] = rope(kr_ref[...]).astype(BF16)

    scale = QK_DIM ** -0.5
    qo = _dot(cqn_scr[...], wq_ref[0])
    q_ref[0, 0, :, :QK_NOPE] = (qo[:, :QK_NOPE] * scale).astype(BF16)
    q_ref[0, 0, :, QK_NOPE:] = (rope(qo[:, QK_NOPE:]) * scale).astype(BF16)
    kvo = _dot(ckvn_scr[...], wkv_ref[0])
    k_ref[0, 0, :, :QK_NOPE] = kvo[:, :QK_NOPE].astype(BF16)
    k_ref[0, 0, :, QK_NOPE:] = kpe_scr[...]
    v_ref[0, 0] = kvo[:, QK_NOPE:].astype(BF16)


def _mla_prep(proj, kr, pos, invf, q_norm, kv_norm, wq, wkv, batch, seq, tm):
    t = proj.shape[0]
    spb = seq // tm
    h = N_HEADS_MLA
    head_map = lambda m, hh: (m // spb, hh, m % spb, 0)
    return pl.pallas_call(
        _mla_prep_kernel,
        grid=(t // tm, h),
        in_specs=[
            pl.BlockSpec((tm, Q_LORA), lambda m, hh: (m, 0)),
            pl.BlockSpec((tm, KV_LORA), lambda m, hh: (m, 1)),
            pl.BlockSpec((tm, 128), lambda m, hh: (m, 0)),
            pl.BlockSpec((tm, 1), lambda m, hh: (m, 0)),
            pl.BlockSpec((1, 128), lambda m, hh: (0, 0)),
            pl.BlockSpec((1, Q_LORA), lambda m, hh: (0, 0)),
            pl.BlockSpec((1, KV_LORA), lambda m, hh: (0, 0)),
            pl.BlockSpec((1, Q_LORA, QK_PAD), lambda m, hh: (hh, 0, 0)),
            pl.BlockSpec((1, KV_LORA, QK_NOPE + V_DIM), lambda m, hh: (hh, 0, 0)),
        ],
        out_specs=[
            pl.BlockSpec((1, 1, tm, QK_PAD), head_map),
            pl.BlockSpec((1, 1, tm, QK_PAD), head_map),
            pl.BlockSpec((1, 1, tm, V_DIM), head_map),
        ],
        out_shape=[
            jax.ShapeDtypeStruct((batch, h, seq, QK_PAD), BF16),
            jax.ShapeDtypeStruct((batch, h, seq, QK_PAD), BF16),
            jax.ShapeDtypeStruct((batch, h, seq, V_DIM), BF16),
        ],
        scratch_shapes=[
            pltpu.VMEM((tm, Q_LORA), BF16),
            pltpu.VMEM((tm, KV_LORA), BF16),
            pltpu.VMEM((tm, 128), F32),
            pltpu.VMEM((tm, 128), BF16),
        ],
        compiler_params=_params("parallel", "arbitrary"),
        name="mla_prep",
    )(proj, proj, kr, pos, invf, q_norm, kv_norm, wq, wkv)


def _attn_kernel(q_ref, k_ref, v_ref, o_ref, m_scr, l_scr, acc_scr, *, tq):
    qi = pl.program_id(2)
    q = q_ref[0, 0]
    m_scr[...] = jnp.full(m_scr.shape, -jnp.inf, F32)
    l_scr[...] = jnp.zeros(l_scr.shape, F32)
    acc_scr[...] = jnp.zeros(acc_scr.shape, F32)

    def step(ki, masked):
        start = pl.multiple_of(ki * tq, tq)
        s = _dot_nt(q, k_ref[0, 0, pl.ds(start, tq), :])
        if masked:
            row = lax.broadcasted_iota(jnp.int32, s.shape, 0)
            col = lax.broadcasted_iota(jnp.int32, s.shape, 1)
            s = jnp.where(col <= row, s, -jnp.inf)
        m_prev = m_scr[...]
        m_new = jnp.maximum(m_prev, jnp.max(s, axis=-1, keepdims=True))
        alpha = jnp.exp(m_prev - m_new)
        p = jnp.exp(s - m_new)
        l_scr[...] = alpha * l_scr[...] + jnp.sum(p, axis=-1, keepdims=True)
        acc_scr[...] = alpha * acc_scr[...] + _dot(p.astype(BF16), v_ref[0, 0, pl.ds(start, tq), :])
        m_scr[...] = m_new

    def body(ki, carry):
        step(ki, False)
        return carry

    lax.fori_loop(0, qi, body, 0)
    step(qi, True)
    o_ref[0] = (acc_scr[...] / l_scr[...]).astype(BF16)


def _attention(q, k, v, tq):
    b, h, s, _ = q.shape
    return pl.pallas_call(
        functools.partial(_attn_kernel, tq=tq),
        grid=(b, h, s // tq),
        in_specs=[
            pl.BlockSpec((1, 1, tq, QK_PAD), lambda bb, hh, qi: (bb, hh, qi, 0)),
            pl.BlockSpec((1, 1, s, QK_PAD), lambda bb, hh, qi: (bb, hh, 0, 0)),
            pl.BlockSpec((1, 1, s, V_DIM), lambda bb, hh, qi: (bb, hh, 0, 0)),
        ],
        out_specs=pl.BlockSpec((1, tq, V_DIM), lambda bb, hh, qi: (bb, qi, hh)),
        out_shape=jax.ShapeDtypeStruct((b, s, h * V_DIM), BF16),
        scratch_shapes=[
            pltpu.VMEM((tq, 1), F32),
            pltpu.VMEM((tq, 1), F32),
            pltpu.VMEM((tq, V_DIM), F32),
        ],
        compiler_params=_params("parallel", "parallel", "arbitrary"),
        name="attention",
    )(q, k, v)


def _split3(x):
    hi = x.astype(BF16).astype(F32)
    r = x - hi
    mid = r.astype(BF16).astype(F32)
    return hi, mid, r - mid


def _ssd_kernel(x_ref, b_ref, c_ref, z_ref, dt_ref, cwx_ref, cwb_ref, cwc_ref, cbx_ref, cbb_ref, cbc_ref,
                dtb_ref, alog_ref, dskip_ref, nw_ref, expand_ref, tri_ref, o_ref,
                state_scr, tx_scr, tb_scr, tc_scr):
    L = x_ref.shape[0]

    @pl.when(pl.program_id(2) == 0)
    def _():
        state_scr[...] = jnp.zeros(state_scr.shape, F32)
        tx_scr[...] = jnp.zeros(tx_scr.shape, F32)
        tb_scr[...] = jnp.zeros(tb_scr.shape, F32)
        tc_scr[...] = jnp.zeros(tc_scr.shape, F32)

    def conv_silu(raw_ref, tail_scr, w_ref, bias_ref):
        raw = raw_ref[...].astype(F32)
        ext = jnp.concatenate([tail_scr[...], raw], axis=0)
        acc = bias_ref[...] + w_ref[CONV_WIDTH - 1:CONV_WIDTH, :] * raw
        for tap in range(CONV_WIDTH - 1):
            off = CONV_TAIL - (CONV_WIDTH - 1) + tap
            acc = acc + w_ref[tap:tap + 1, :] * ext[off:off + L]
        tail_scr[...] = raw[L - CONV_TAIL:, :]
        return acc * _sigmoid(acc)

    xc = conv_silu(x_ref, tx_scr, cwx_ref, cbx_ref)
    bc = conv_silu(b_ref, tb_scr, cwb_ref, cbb_ref)
    cc = conv_silu(c_ref, tc_scr, cwc_ref, cbc_ref)

    dtr = dt_ref[...] + dtb_ref[...]
    dt = jnp.maximum(dtr, 0.0) + jnp.log1p(jnp.exp(-jnp.abs(dtr)))
    da = dt * (-jnp.exp(alog_ref[...]))
    d_hi, d_mid, d_lo = _split3(da)
    tri = tri_ref[...]
    cum = _dot(d_hi.astype(BF16), tri) + _dot(d_mid.astype(BF16), tri) + _dot(d_lo.astype(BF16), tri)

    t_hi, t_mid, t_lo = _split3(dt)
    c_hi, c_mid, c_lo = _split3(cum)
    stacked = jnp.concatenate(
        [t_hi, t_mid, t_lo, c_hi, c_mid, c_lo, cum, jnp.zeros((128 - 7 * HEADS_PER_GROUP, L), F32)], axis=0)
    cols = stacked.T
    expanded = _dot(cols.astype(BF16), expand_ref[...])
    dt_e = expanded[:, :GROUP_WIDTH]
    cum_e = expanded[:, GROUP_WIDTH:]

    xdt = xc * dt_e
    xdt_b = xdt.astype(BF16)
    bc_b = bc.astype(BF16)
    cc_b = cc.astype(BF16)
    last_e = cum_e[L - 1:L, :]
    state = state_scr[...]

    y = _dot(cc_b, state.astype(BF16)) * jnp.exp(cum_e)
    y = y + dskip_ref[...] * xc

    w_end = jnp.exp(last_e - cum_e)
    state_scr[...] = state * jnp.exp(last_e) + _dot(bc.T.astype(BF16), (xdt * w_end).astype(BF16))

    cb = _dot_nt(cc_b, bc_b)
    ri = lax.broadcasted_iota(jnp.int32, (L, L), 0)
    ci = lax.broadcasted_iota(jnp.int32, (L, L), 1)
    causal = ci <= ri
    lane = lax.broadcasted_iota(jnp.int32, (L, 128), 1)
    pairs = []
    for pr in range(HEADS_PER_GROUP // 2):
        xp = xdt_b[:, pr * 128:(pr + 1) * 128]
        acc = None
        for sub in range(2):
            r = 2 * pr + sub
            col = cols[:, 6 * HEADS_PER_GROUP + r:6 * HEADS_PER_GROUP + r + 1]
            seg = col - cum[r:r + 1, :]
            m = (cb * jnp.exp(jnp.where(causal, seg, -jnp.inf))).astype(BF16)
            keep = (lane < SSM_HEADDIM) if sub == 0 else (lane >= SSM_HEADDIM)
            part = _dot(m, jnp.where(keep, xp, jnp.zeros_like(xp)))
            acc = part if acc is None else acc + part
        pairs.append(acc)
    y = y + jnp.concatenate(pairs, axis=1)

    zf = z_ref[...].astype(F32)
    y = y * (zf * _sigmoid(zf))
    o_ref[...] = _rms(y, nw_ref[...]).astype(BF16)


def _ssd(proj, dt_t, conv_w, conv_b, dt_bias, a_log, dskip_e, ssm_norm, expand, tri, batch, seq, offs):
    t = proj.shape[0]
    L = CHUNK
    nc = seq // L
    g = SSM_GROUPS
    d_inner = g * GROUP_WIDTH
    row = lambda bb, gg, c: bb * nc + c
    z_blk, x_blk, b_blk, c_blk = offs
    return pl.pallas_call(
        _ssd_kernel,
        grid=(batch, g, nc),
        in_specs=[
            pl.BlockSpec((L, GROUP_WIDTH), lambda bb, gg, c: (row(bb, gg, c), x_blk + gg)),
            pl.BlockSpec((L, D_STATE), lambda bb, gg, c: (row(bb, gg, c), b_blk + gg)),
            pl.BlockSpec((L, D_STATE), lambda bb, gg, c: (row(bb, gg, c), c_blk + gg)),
            pl.BlockSpec((L, GROUP_WIDTH), lambda bb, gg, c: (row(bb, gg, c), z_blk + gg)),
            pl.BlockSpec((HEADS_PER_GROUP, L), lambda bb, gg, c: (gg, row(bb, gg, c))),
            pl.BlockSpec((CONV_WIDTH, GROUP_WIDTH), lambda bb, gg, c: (0, gg)),
            pl.BlockSpec((CONV_WIDTH, D_STATE), lambda bb, gg, c: (0, d_inner // D_STATE + gg)),
            pl.BlockSpec((CONV_WIDTH, D_STATE), lambda bb, gg, c: (0, d_inner // D_STATE + g + gg)),
            pl.BlockSpec((1, GROUP_WIDTH), lambda bb, gg, c: (0, gg)),
            pl.BlockSpec((1, D_STATE), lambda bb, gg, c: (0, d_inner // D_STATE + gg)),
            pl.BlockSpec((1, D_STATE), lambda bb, gg, c: (0, d_inner // D_STATE + g + gg)),
            pl.BlockSpec((HEADS_PER_GROUP, 1), lambda bb, gg, c: (gg, 0)),
            pl.BlockSpec((HEADS_PER_GROUP, 1), lambda bb, gg, c: (gg, 0)),
            pl.BlockSpec((1, GROUP_WIDTH), lambda bb, gg, c: (0, gg)),
            pl.BlockSpec((1, GROUP_WIDTH), lambda bb, gg, c: (0, gg)),
            pl.BlockSpec((128, 2 * GROUP_WIDTH), lambda bb, gg, c: (0, 0)),
            pl.BlockSpec((L, L), lambda bb, gg, c: (0, 0)),
        ],
        out_specs=pl.BlockSpec((L, GROUP_WIDTH), lambda bb, gg, c: (row(bb, gg, c), gg)),
        out_shape=jax.ShapeDtypeStruct((t, d_inner), BF16),
        scratch_shapes=[
            pltpu.VMEM((D_STATE, GROUP_WIDTH), F32),
            pltpu.VMEM((CONV_TAIL, GROUP_WIDTH), F32),
            pltpu.VMEM((CONV_TAIL, D_STATE), F32),
            pltpu.VMEM((CONV_TAIL, D_STATE), F32),
        ],
        compiler_params=_params("parallel", "parallel", "arbitrary"),
        name="ssd",
    )(proj, proj, proj, proj, dt_t, conv_w, conv_w, conv_w, conv_b, conv_b, conv_b,
      dt_bias, a_log, dskip_e, ssm_norm, expand, tri)


def _mix_kernel(attn_ref, ssm_ref, ga_ref, gs_ref, wa_ref, ws_ref, o_ref):
    a = _dot(attn_ref[...], wa_ref[...])
    s = _dot(ssm_ref[...], ws_ref[...])
    mixed = _sigmoid(ga_ref[...].astype(F32)) * a + _sigmoid(gs_ref[...].astype(F32)) * s
    o_ref[...] = mixed.astype(BF16)


def _mix(attn, ssm, proj, wa, ws, ga_blk, gs_blk, tm, tn):
    t = attn.shape[0]
    d = wa.shape[1]
    return pl.pallas_call(
        _mix_kernel,
        grid=(t // tm, d // tn),
        in_specs=[
            pl.BlockSpec((tm, attn.shape[1]), lambda m, n: (m, 0)),
            pl.BlockSpec((tm, ssm.shape[1]), lambda m, n: (m, 0)),
            pl.BlockSpec((tm, tn), lambda m, n: (m, ga_blk + n)),
            pl.BlockSpec((tm, tn), lambda m, n: (m, gs_blk + n)),
            pl.BlockSpec((wa.shape[0], tn), lambda m, n: (0, n)),
            pl.BlockSpec((ws.shape[0], tn), lambda m, n: (0, n)),
        ],
        out_specs=pl.BlockSpec((tm, tn), lambda m, n: (m, n)),
        out_shape=jax.ShapeDtypeStruct((t, d), BF16),
        compiler_params=_params("parallel", "arbitrary"),
        name="mix",
    )(attn, ssm, proj, proj, wa, ws)


def _mix_out_kernel(mixed_ref, x_ref, w_ref, post_ref, pre_ref, h_ref, f_ref):
    o = _dot(mixed_ref[...], w_ref[...])
    h = x_ref[...] + _rms(o, post_ref[...])
    h_ref[...] = h
    f_ref[...] = _rms(h, pre_ref[...]).astype(BF16)


def _mix_out(mixed, x2, w_out, post_w, pre_w, tm):
    t, d = x2.shape
    row = pl.BlockSpec((tm, d), lambda m: (m, 0))
    vec = pl.BlockSpec((1, d), lambda m: (0, 0))
    return pl.pallas_call(
        _mix_out_kernel,
        grid=(t // tm,),
        in_specs=[row, row, pl.BlockSpec((d, d), lambda m: (0, 0)), vec, vec],
        out_specs=[row, row],
        out_shape=[jax.ShapeDtypeStruct((t, d), F32), jax.ShapeDtypeStruct((t, d), BF16)],
        compiler_params=_params("parallel"),
        name="mix_out",
    )(mixed, x2, w_out, post_w, pre_w)


def _ffn_kernel(f_ref, wg_ref, wu_ref, wd_ref, o_ref, acc_scr):
    kf = pl.program_id(1)

    @pl.when(kf == 0)
    def _():
        acc_scr[...] = jnp.zeros(acc_scr.shape, F32)

    f = f_ref[...]
    g = _dot(f, wg_ref[...])
    u = _dot(f, wu_ref[...])
    a = (g * _sigmoid(g) * u).astype(BF16)
    acc_scr[...] += _dot(a, wd_ref[...])

    @pl.when(kf == pl.num_programs(1) - 1)
    def _():
        o_ref[...] = acc_scr[...].astype(BF16)


def _ffn(f, wg, wu, wd, tm, tf):
    t, d = f.shape
    dff = wg.shape[1]
    return pl.pallas_call(
        _ffn_kernel,
        grid=(t // tm, dff // tf),
        in_specs=[
            pl.BlockSpec((tm, d), lambda m, k: (m, 0)),
            pl.BlockSpec((d, tf), lambda m, k: (0, k)),
            pl.BlockSpec((d, tf), lambda m, k: (0, k)),
            pl.BlockSpec((tf, d), lambda m, k: (k, 0)),
        ],
        out_specs=pl.BlockSpec((tm, d), lambda m, k: (m, 0)),
        out_shape=jax.ShapeDtypeStruct((t, d), BF16),
        scratch_shapes=[pltpu.VMEM((tm, d), F32)],
        compiler_params=_params("parallel", "arbitrary"),
        name="ffn",
    )(f, wg, wu, wd)


def _ple_kernel(h_ref, d_ref, p_ref, fpost_ref, ppre_ref, ppost_ref, wg_ref, wp_ref, o_ref):
    h = h_ref[...] + _rms(d_ref[...].astype(F32), fpost_ref[...])
    hn = _rms(h, ppre_ref[...]).astype(BF16)
    gate = _sigmoid(_dot(hn, wg_ref[...]))
    e = _dot(p_ref[...].astype(BF16), wp_ref[...]) * gate
    o_ref[...] = h + _rms(e, ppost_ref[...])


def _ple(h1, dff, p2, fpost, ppre, ppost, wg, wp, tm):
    t, d = h1.shape
    row = pl.BlockSpec((tm, d), lambda m: (m, 0))
    vec = pl.BlockSpec((1, d), lambda m: (0, 0))
    return pl.pallas_call(
        _ple_kernel,
        grid=(t // tm,),
        in_specs=[row, row, pl.BlockSpec((tm, p2.shape[1]), lambda m: (m, 0)), vec, vec, vec,
                  pl.BlockSpec((d, d), lambda m: (0, 0)),
                  pl.BlockSpec((p2.shape[1], d), lambda m: (0, 0))],
        out_specs=row,
        out_shape=jax.ShapeDtypeStruct((t, d), F32),
        compiler_params=_params("parallel"),
        name="ple",
    )(h1, dff, p2, fpost, ppre, ppost, wg, wp)


def _expand_matrix():
    e = np.zeros((128, 2 * GROUP_WIDTH), np.float32)
    for part in range(3):
        for r in range(HEADS_PER_GROUP):
            lo = r * SSM_HEADDIM
            e[part * HEADS_PER_GROUP + r, lo:lo + SSM_HEADDIM] = 1.0
            e[(3 + part) * HEADS_PER_GROUP + r, GROUP_WIDTH + lo:GROUP_WIDTH + lo + SSM_HEADDIM] = 1.0
    return e


def _layer(h, p_i, pos, invf, mix_norm_pre, mix_norm_post, w_in, q_norm, w_uq, kv_norm, w_ukv,
           conv_w, conv_b, dt_bias, a_log, d_skip, ssm_norm, w_attn_o, w_ssm_o, w_out,
           ffn_norm_pre, ffn_norm_post, w_gate, w_up, w_down,
           ple_norm_pre, ple_norm_post, w_ple_gate, w_ple):
    batch, seq, d = h.shape
    t = batch * seq
    d_inner = SSM_GROUPS * GROUP_WIDTH
    bc_width = SSM_GROUPS * D_STATE
    n_ssm_heads = SSM_GROUPS * HEADS_PER_GROUP
    vec = lambda v: v.reshape(1, -1).astype(F32)

    o_kr = Q_LORA + KV_LORA
    o_z = o_kr + QK_ROPE
    o_xbc = o_z + d_inner
    o_dt = o_xbc + d_inner + 2 * bc_width
    o_ga = o_dt + n_ssm_heads
    w_main = jnp.concatenate([w_in[:, :o_kr], w_in[:, o_z:o_dt], w_in[:, o_ga:]], axis=1).astype(BF16)
    half = QK_ROPE // 2
    w_kr = jnp.concatenate([w_in[:, o_kr:o_z], w_in[:, o_kr + half:o_z], w_in[:, o_kr:o_kr + half]],
                           axis=1).astype(BF16)
    w_dt_t = w_in[:, o_dt:o_ga].T.astype(BF16)
    z_blk = (Q_LORA + KV_LORA) // GROUP_WIDTH
    x_blk = z_blk + SSM_GROUPS
    b_blk = (Q_LORA + KV_LORA + 2 * d_inner) // D_STATE
    c_blk = b_blk + SSM_GROUPS
    g_off = Q_LORA + KV_LORA + 2 * d_inner + 2 * bc_width

    x2 = h.reshape(t, d)
    proj, kr, dt_t = _inproj(x2, vec(mix_norm_pre), w_main, w_kr, w_dt_t, tm=512, tn=1024)

    wq = w_uq.reshape(Q_LORA, N_HEADS_MLA, QK_DIM)
    wq = jnp.concatenate([wq, wq[:, :, QK_NOPE + half:], wq[:, :, QK_NOPE:QK_NOPE + half]], axis=2)
    wq = jnp.transpose(wq, (1, 0, 2)).astype(BF16)
    wkv = jnp.transpose(w_ukv.reshape(KV_LORA, N_HEADS_MLA, QK_NOPE + V_DIM), (1, 0, 2)).astype(BF16)
    q, k, v = _mla_prep(proj, kr, pos, invf, vec(q_norm), vec(kv_norm), wq, wkv, batch, seq, tm=512)
    attn = _attention(q, k, v, tq=512).reshape(t, N_HEADS_MLA * V_DIM)

    expand = jnp.asarray(_expand_matrix(), BF16)
    tri = jnp.asarray(np.triu(np.ones((CHUNK, CHUNK), np.float32)), BF16)
    dskip_e = jnp.repeat(d_skip.astype(F32), SSM_HEADDIM).reshape(1, d_inner)
    ssm = _ssd(proj, dt_t, conv_w.astype(F32), vec(conv_b), dt_bias.reshape(-1, 1).astype(F32),
               a_log.reshape(-1, 1).astype(F32), dskip_e, vec(ssm_norm), expand, tri, batch, seq,
               (z_blk, x_blk, b_blk, c_blk))

    tn_mix = 512
    mixed = _mix(attn, ssm, proj, w_attn_o.astype(BF16), w_ssm_o.astype(BF16),
                 g_off // tn_mix, (g_off + d) // tn_mix, tm=1024, tn=tn_mix)
    h1, f = _mix_out(mixed, x2, w_out.astype(BF16), vec(mix_norm_post), vec(ffn_norm_pre), tm=512)
    dff = _ffn(f, w_gate.astype(BF16), w_up.astype(BF16), w_down.astype(BF16), tm=1024, tf=512)
    out = _ple(h1, dff, p_i.reshape(t, -1), vec(ffn_norm_post), vec(ple_norm_pre), vec(ple_norm_post),
               w_ple_gate.astype(BF16), w_ple.astype(BF16), tm=512)
    return out.reshape(batch, seq, d)


def kernel(x, p, positions, mix_norm_pre, mix_norm_post, w_in, q_norm, w_uq, kv_norm, w_ukv, conv_w, conv_b, dt_bias, a_log, d_skip, ssm_norm, w_attn_o, w_ssm_o, w_out, ffn_norm_pre, ffn_norm_post, w_gate, w_up, w_down, ple_norm_pre, ple_norm_post, w_ple_gate, w_ple):
    invf = ROPE_THETA ** (-jnp.arange(0, QK_ROPE, 2, dtype=F32) / QK_ROPE)
    invf = jnp.tile(invf, 128 // invf.shape[0]).reshape(1, 128)
    pos = positions.reshape(-1, 1)
    h = x
    for i in range(w_in.shape[0]):
        h = _layer(h, p[i], pos, invf, mix_norm_pre[i], mix_norm_post[i], w_in[i], q_norm[i], w_uq[i],
                   kv_norm[i], w_ukv[i], conv_w[i], conv_b[i], dt_bias[i], a_log[i], d_skip[i], ssm_norm[i],
                   w_attn_o[i], w_ssm_o[i], w_out[i], ffn_norm_pre[i], ffn_norm_post[i],
                   w_gate[i], w_up[i], w_down[i], ple_norm_pre[i], ple_norm_post[i], w_ple_gate[i], w_ple[i])
    return h
```

```python
import functools

import jax
import jax.numpy as jnp
import numpy as np
from jax import lax
from jax.experimental import pallas as pl
from jax.experimental.pallas import tpu as pltpu

F32 = jnp.float32
BF16 = jnp.bfloat16

EPS = 1e-6
ROPE_THETA = 10000.0
LOG2_E = 1.4426950408889634

N_HEADS_MLA = 16
Q_LORA = 512
KV_LORA = 512
QK_NOPE = 128
QK_ROPE = 64
V_DIM = 128
QK_DIM = QK_NOPE + QK_ROPE
QK_PAD = 256

SSM_HEADDIM = 64
SSM_GROUPS = 8
HEADS_PER_GROUP = 8
GROUP_WIDTH = HEADS_PER_GROUP * SSM_HEADDIM
D_STATE = 128
CONV_WIDTH = 4
CHUNK = 256
CONV_TAIL = 8

VMEM_LIMIT = 56 * 1024 * 1024


def _params(*sem):
    return pltpu.CompilerParams(dimension_semantics=sem, vmem_limit_bytes=VMEM_LIMIT)


def _rms(x, w):
    return x * lax.rsqrt(jnp.mean(x * x, axis=-1, keepdims=True) + EPS) * w


def _sigmoid(x):
    return 1.0 / (1.0 + jnp.exp(-x))


def _dot(a, b):
    return jnp.dot(a, b, preferred_element_type=F32)


def _dot_nt(a, b):
    return lax.dot_general(a, b, (((1,), (1,)), ((), ())), preferred_element_type=F32)


def _inproj_kernel(x_ref, nw_ref, w_ref, wt_ref, o_ref, ot_ref, u_scr):
    @pl.when(pl.program_id(1) == 0)
    def _():
        u = _rms(x_ref[...], nw_ref[...]).astype(BF16)
        u_scr[...] = u
        ot_ref[...] = _dot_nt(wt_ref[...], u)

    o_ref[...] = _dot(u_scr[...], w_ref[...]).astype(BF16)


def _inproj(x2, norm_w, w_main, w_small_t, tm, tn):
    t, d = x2.shape
    n_main = w_main.shape[1]
    n_small = w_small_t.shape[0]
    return pl.pallas_call(
        _inproj_kernel,
        grid=(t // tm, n_main // tn),
        in_specs=[
            pl.BlockSpec((tm, d), lambda m, n: (m, 0)),
            pl.BlockSpec((1, d), lambda m, n: (0, 0)),
            pl.BlockSpec((d, tn), lambda m, n: (0, n)),
            pl.BlockSpec((n_small, d), lambda m, n: (0, 0)),
        ],
        out_specs=[
            pl.BlockSpec((tm, tn), lambda m, n: (m, n)),
            pl.BlockSpec((n_small, tm), lambda m, n: (0, m)),
        ],
        out_shape=[
            jax.ShapeDtypeStruct((t, n_main), BF16),
            jax.ShapeDtypeStruct((n_small, t), F32),
        ],
        scratch_shapes=[pltpu.VMEM((tm, d), BF16)],
        compiler_params=_params("parallel", "arbitrary"),
        name="inproj",
    )(x2, norm_w, w_main, w_small_t)


def _mla_prep_kernel(cq_ref, ckv_ref, kr_ref, pos_ref, invf_ref, qn_ref, kvn_ref, wq_ref, wk_ref, wv_ref,
                     qt_ref, k_ref, vt_ref, cqn_scr, ckvn_scr, tab_scr, kpe_scr):
    def rope_t(pair):
        t = pair * tab_scr[...]
        return t[:QK_ROPE] + t[QK_ROPE:]

    @pl.when(pl.program_id(1) == 0)
    def _():
        cqn_scr[...] = _rms(cq_ref[...].astype(F32), qn_ref[...]).astype(BF16)
        ckvn_scr[...] = _rms(ckv_ref[...].astype(F32), kvn_ref[...]).astype(BF16)
        ang = invf_ref[...] * pos_ref[...].astype(F32)
        row = lax.broadcasted_iota(jnp.int32, ang.shape, 0)
        sin = jnp.sin(ang)
        neg = (row >= QK_ROPE) & (row < QK_ROPE + QK_ROPE // 2)
        tab_scr[...] = jnp.where(row < QK_ROPE, jnp.cos(ang), jnp.where(neg, -sin, sin))
        kpe_t = rope_t(kr_ref[...])
        kpe_scr[...] = jnp.concatenate([kpe_t, jnp.zeros_like(kpe_t)], axis=0).T.astype(BF16)

    scale = QK_DIM ** -0.5 * LOG2_E
    qo = _dot_nt(wq_ref[0], cqn_scr[...])
    qt_ref[0, 0, :QK_NOPE, :] = (qo[:QK_NOPE] * scale).astype(BF16)
    q_pe = (rope_t(qo[QK_NOPE:]) * scale).astype(BF16)
    qt_ref[0, 0, QK_NOPE:QK_DIM, :] = q_pe
    qt_ref[0, 0, QK_DIM:, :] = jnp.zeros_like(q_pe)
    ckvn = ckvn_scr[...]
    k_ref[0, 0, :, :QK_NOPE] = _dot(ckvn, wk_ref[0]).astype(BF16)
    k_ref[0, 0, :, QK_NOPE:] = kpe_scr[...]
    vt_ref[0, 0] = _dot_nt(wv_ref[0], ckvn).astype(BF16)


def _mla_prep(proj, small_t, pos, invf, q_norm, kv_norm, wq_t, wk, wv_t, batch, seq, tm):
    t = proj.shape[0]
    spb = seq // tm
    h = N_HEADS_MLA
    return pl.pallas_call(
        _mla_prep_kernel,
        grid=(t // tm, h),
        in_specs=[
            pl.BlockSpec((tm, Q_LORA), lambda m, hh: (m, 0)),
            pl.BlockSpec((tm, KV_LORA), lambda m, hh: (m, 1)),
            pl.BlockSpec((2 * QK_ROPE, tm), lambda m, hh: (0, m)),
            pl.BlockSpec((1, tm), lambda m, hh: (0, m)),
            pl.BlockSpec((2 * QK_ROPE, 1), lambda m, hh: (0, 0)),
            pl.BlockSpec((1, Q_LORA), lambda m, hh: (0, 0)),
            pl.BlockSpec((1, KV_LORA), lambda m, hh: (0, 0)),
            pl.BlockSpec((1, QK_PAD, Q_LORA), lambda m, hh: (hh, 0, 0)),
            pl.BlockSpec((1, KV_LORA, QK_NOPE), lambda m, hh: (hh, 0, 0)),
            pl.BlockSpec((1, V_DIM, KV_LORA), lambda m, hh: (hh, 0, 0)),
        ],
        out_specs=[
            pl.BlockSpec((1, 1, QK_PAD, tm), lambda m, hh: (m // spb, hh, 0, m % spb)),
            pl.BlockSpec((1, 1, tm, QK_PAD), lambda m, hh: (m // spb, hh, m % spb, 0)),
            pl.BlockSpec((1, 1, V_DIM, tm), lambda m, hh: (m // spb, hh, 0, m % spb)),
        ],
        out_shape=[
            jax.ShapeDtypeStruct((batch, h, QK_PAD, seq), BF16),
            jax.ShapeDtypeStruct((batch, h, seq, QK_PAD), BF16),
            jax.ShapeDtypeStruct((batch, h, V_DIM, seq), BF16),
        ],
        scratch_shapes=[
            pltpu.VMEM((tm, Q_LORA), BF16),
            pltpu.VMEM((tm, KV_LORA), BF16),
            pltpu.VMEM((2 * QK_ROPE, tm), F32),
            pltpu.VMEM((tm, 2 * QK_ROPE), BF16),
        ],
        compiler_params=_params("parallel", "arbitrary"),
        name="mla_prep",
    )(proj, proj, small_t, pos, invf, q_norm, kv_norm, wq_t, wk, wv_t)


def _attn_kernel(qt_ref, k_ref, vt_ref, o_ref, m_scr, l_scr, acc_scr, *, tq):
    qi = pl.program_id(2)
    m_scr[...] = jnp.full(m_scr.shape, -jnp.inf, F32)
    l_scr[...] = jnp.zeros(l_scr.shape, F32)
    acc_scr[...] = jnp.zeros(acc_scr.shape, F32)

    def step(ki, masked):
        start = pl.multiple_of(ki * tq, tq)
        s = _dot(k_ref[0, 0, pl.ds(start, tq), :], qt_ref[0, 0])
        if masked:
            key = lax.broadcasted_iota(jnp.int32, s.shape, 0)
            qry = lax.broadcasted_iota(jnp.int32, s.shape, 1)
            s = jnp.where(key <= qry, s, -jnp.inf)
        m_prev = m_scr[...]
        m_new = jnp.maximum(m_prev, jnp.max(s, axis=0, keepdims=True))
        alpha = jnp.exp2(m_prev - m_new)
        p = jnp.exp2(s - m_new)
        l_scr[...] = alpha * l_scr[...] + jnp.sum(p, axis=0, keepdims=True)
        acc_scr[...] = alpha * acc_scr[...] + _dot(vt_ref[0, 0, :, pl.ds(start, tq)], p.astype(BF16))
        m_scr[...] = m_new

    def body(ki, carry):
        step(ki, False)
        return carry

    lax.fori_loop(0, qi, body, 0)
    step(qi, True)
    o_ref[0] = (acc_scr[...] / l_scr[...]).T.astype(BF16)


def _attention(qt, k, vt, tq):
    b, h, s, _ = k.shape
    return pl.pallas_call(
        functools.partial(_attn_kernel, tq=tq),
        grid=(b, h, s // tq),
        in_specs=[
            pl.BlockSpec((1, 1, QK_PAD, tq), lambda bb, hh, qi: (bb, hh, 0, qi)),
            pl.BlockSpec((1, 1, s, QK_PAD), lambda bb, hh, qi: (bb, hh, 0, 0)),
            pl.BlockSpec((1, 1, V_DIM, s), lambda bb, hh, qi: (bb, hh, 0, 0)),
        ],
        out_specs=pl.BlockSpec((1, tq, V_DIM), lambda bb, hh, qi: (bb, qi, hh)),
        out_shape=jax.ShapeDtypeStruct((b, s, h * V_DIM), BF16),
        scratch_shapes=[
            pltpu.VMEM((1, tq), F32),
            pltpu.VMEM((1, tq), F32),
            pltpu.VMEM((V_DIM, tq), F32),
        ],
        compiler_params=_params("parallel", "parallel", "arbitrary"),
        name="attention",
    )(qt, k, vt)


def _split3(x):
    hi = x.astype(BF16).astype(F32)
    r = x - hi
    mid = r.astype(BF16).astype(F32)
    return hi, mid, r - mid


def _ssd_kernel(x_ref, b_ref, c_ref, z_ref, dt_ref, cwx_ref, cwb_ref, cwc_ref, cbx_ref, cbb_ref, cbc_ref,
                dtb_ref, alog_ref, dskip_ref, nw_ref, expand_ref, tri_ref, o_ref,
                state_scr, tx_scr, tb_scr, tc_scr):
    L = x_ref.shape[0]

    @pl.when(pl.program_id(2) == 0)
    def _():
        state_scr[...] = jnp.zeros(state_scr.shape, F32)
        tx_scr[...] = jnp.zeros(tx_scr.shape, F32)
        tb_scr[...] = jnp.zeros(tb_scr.shape, F32)
        tc_scr[...] = jnp.zeros(tc_scr.shape, F32)

    def conv_silu(raw_ref, tail_scr, w_ref, bias_ref):
        raw = raw_ref[...].astype(F32)
        ext = jnp.concatenate([tail_scr[...], raw], axis=0)
        acc = bias_ref[...] + w_ref[CONV_WIDTH - 1:CONV_WIDTH, :] * raw
        for tap in range(CONV_WIDTH - 1):
            off = CONV_TAIL - (CONV_WIDTH - 1) + tap
            acc = acc + w_ref[tap:tap + 1, :] * ext[off:off + L]
        tail_scr[...] = raw[L - CONV_TAIL:, :]
        return acc * _sigmoid(acc)

    xc = conv_silu(x_ref, tx_scr, cwx_ref, cbx_ref)
    bc = conv_silu(b_ref, tb_scr, cwb_ref, cbb_ref)
    cc = conv_silu(c_ref, tc_scr, cwc_ref, cbc_ref)

    dtr = dt_ref[...] + dtb_ref[...]
    dt = jnp.maximum(dtr, 0.0) + jnp.log1p(jnp.exp(-jnp.abs(dtr)))
    da = dt * (-jnp.exp(alog_ref[...]))
    d_hi, d_mid, d_lo = _split3(da)
    tri = tri_ref[...]
    cum = _dot(d_hi.astype(BF16), tri) + _dot(d_mid.astype(BF16), tri) + _dot(d_lo.astype(BF16), tri)

    t_hi, t_mid, t_lo = _split3(dt)
    c_hi, c_mid, c_lo = _split3(cum)
    stacked = jnp.concatenate(
        [t_hi, t_mid, t_lo, c_hi, c_mid, c_lo, cum, jnp.zeros((128 - 7 * HEADS_PER_GROUP, L), F32)], axis=0)
    cols = stacked.T
    expanded = _dot(cols.astype(BF16), expand_ref[...])
    dt_e = expanded[:, :GROUP_WIDTH]
    cum_e = expanded[:, GROUP_WIDTH:]

    xdt = xc * dt_e
    xdt_b = xdt.astype(BF16)
    bc_b = bc.astype(BF16)
    cc_b = cc.astype(BF16)
    last_e = cum_e[L - 1:L, :]
    state = state_scr[...]

    y = _dot(cc_b, state.astype(BF16)) * jnp.exp(cum_e)
    y = y + dskip_ref[...] * xc

    w_end = jnp.exp(last_e - cum_e)
    state_scr[...] = state * jnp.exp(last_e) + _dot(bc.T.astype(BF16), (xdt * w_end).astype(BF16))

    cb = _dot_nt(cc_b, bc_b)
    ri = lax.broadcasted_iota(jnp.int32, (L, L), 0)
    ci = lax.broadcasted_iota(jnp.int32, (L, L), 1)
    causal = ci <= ri
    lane = lax.broadcasted_iota(jnp.int32, (L, 128), 1)
    pairs = []
    for pr in range(HEADS_PER_GROUP // 2):
        xp = xdt_b[:, pr * 128:(pr + 1) * 128]
        acc = None
        for sub in range(2):
            r = 2 * pr + sub
            col = cols[:, 6 * HEADS_PER_GROUP + r:6 * HEADS_PER_GROUP + r + 1]
            seg = col - cum[r:r + 1, :]
            m = (cb * jnp.exp(jnp.where(causal, seg, -jnp.inf))).astype(BF16)
            keep = (lane < SSM_HEADDIM) if sub == 0 else (lane >= SSM_HEADDIM)
            part = _dot(m, jnp.where(keep, xp, jnp.zeros_like(xp)))
            acc = part if acc is None else acc + part
        pairs.append(acc)
    y = y + jnp.concatenate(pairs, axis=1)

    zf = z_ref[...].astype(F32)
    y = y * (zf * _sigmoid(zf))
    o_ref[...] = _rms(y, nw_ref[...]).astype(BF16)


def _ssd(proj, dt_t, conv_w, conv_b, dt_bias, a_log, dskip_e, ssm_norm, expand, tri, batch, seq, offs):
    t = proj.shape[0]
    L = CHUNK
    nc = seq // L
    g = SSM_GROUPS
    d_inner = g * GROUP_WIDTH
    row = lambda bb, gg, c: bb * nc + c
    z_blk, x_blk, b_blk, c_blk, dt_blk = offs
    return pl.pallas_call(
        _ssd_kernel,
        grid=(batch, g, nc),
        in_specs=[
            pl.BlockSpec((L, GROUP_WIDTH), lambda bb, gg, c: (row(bb, gg, c), x_blk + gg)),
            pl.BlockSpec((L, D_STATE), lambda bb, gg, c: (row(bb, gg, c), b_blk + gg)),
            pl.BlockSpec((L, D_STATE), lambda bb, gg, c: (row(bb, gg, c), c_blk + gg)),
            pl.BlockSpec((L, GROUP_WIDTH), lambda bb, gg, c: (row(bb, gg, c), z_blk + gg)),
            pl.BlockSpec((HEADS_PER_GROUP, L), lambda bb, gg, c: (dt_blk + gg, row(bb, gg, c))),
            pl.BlockSpec((CONV_WIDTH, GROUP_WIDTH), lambda bb, gg, c: (0, gg)),
            pl.BlockSpec((CONV_WIDTH, D_STATE), lambda bb, gg, c: (0, d_inner // D_STATE + gg)),
            pl.BlockSpec((CONV_WIDTH, D_STATE), lambda bb, gg, c: (0, d_inner // D_STATE + g + gg)),
            pl.BlockSpec((1, GROUP_WIDTH), lambda bb, gg, c: (0, gg)),
            pl.BlockSpec((1, D_STATE), lambda bb, gg, c: (0, d_inner // D_STATE + gg)),
            pl.BlockSpec((1, D_STATE), lambda bb, gg, c: (0, d_inner // D_STATE + g + gg)),
            pl.BlockSpec((HEADS_PER_GROUP, 1), lambda bb, gg, c: (gg, 0)),
            pl.BlockSpec((HEADS_PER_GROUP, 1), lambda bb, gg, c: (gg, 0)),
            pl.BlockSpec((1, GROUP_WIDTH), lambda bb, gg, c: (0, gg)),
            pl.BlockSpec((1, GROUP_WIDTH), lambda bb, gg, c: (0, gg)),
            pl.BlockSpec((128, 2 * GROUP_WIDTH), lambda bb, gg, c: (0, 0)),
            pl.BlockSpec((L, L), lambda bb, gg, c: (0, 0)),
        ],
        out_specs=pl.BlockSpec((L, GROUP_WIDTH), lambda bb, gg, c: (row(bb, gg, c), gg)),
        out_shape=jax.ShapeDtypeStruct((t, d_inner), BF16),
        scratch_shapes=[
            pltpu.VMEM((D_STATE, GROUP_WIDTH), F32),
            pltpu.VMEM((CONV_TAIL, GROUP_WIDTH), F32),
            pltpu.VMEM((CONV_TAIL, D_STATE), F32),
            pltpu.VMEM((CONV_TAIL, D_STATE), F32),
        ],
        compiler_params=_params("parallel", "parallel", "arbitrary"),
        name="ssd",
    )(proj, proj, proj, proj, dt_t, conv_w, conv_w, conv_w, conv_b, conv_b, conv_b,
      dt_bias, a_log, dskip_e, ssm_norm, expand, tri)


def _mix_kernel(attn_ref, ssm_ref, ga_ref, gs_ref, wa_ref, ws_ref, o_ref):
    a = _dot(attn_ref[...], wa_ref[...])
    s = _dot(ssm_ref[...], ws_ref[...])
    mixed = _sigmoid(ga_ref[...].astype(F32)) * a + _sigmoid(gs_ref[...].astype(F32)) * s
    o_ref[...] = mixed.astype(BF16)


def _mix(attn, ssm, proj, wa, ws, ga_blk, gs_blk, tm, tn):
    t = attn.shape[0]
    d = wa.shape[1]
    return pl.pallas_call(
        _mix_kernel,
        grid=(t // tm, d // tn),
        in_specs=[
            pl.BlockSpec((tm, attn.shape[1]), lambda m, n: (m, 0)),
            pl.BlockSpec((tm, ssm.shape[1]), lambda m, n: (m, 0)),
            pl.BlockSpec((tm, tn), lambda m, n: (m, ga_blk + n)),
            pl.BlockSpec((tm, tn), lambda m, n: (m, gs_blk + n)),
            pl.BlockSpec((wa.shape[0], tn), lambda m, n: (0, n)),
            pl.BlockSpec((ws.shape[0], tn), lambda m, n: (0, n)),
        ],
        out_specs=pl.BlockSpec((tm, tn), lambda m, n: (m, n)),
        out_shape=jax.ShapeDtypeStruct((t, d), BF16),
        compiler_params=_params("parallel", "arbitrary"),
        name="mix",
    )(attn, ssm, proj, proj, wa, ws)


def _mix_out_kernel(mixed_ref, x_ref, w_ref, post_ref, pre_ref, h_ref, f_ref):
    o = _dot(mixed_ref[...], w_ref[...])
    h = x_ref[...] + _rms(o, post_ref[...])
    h_ref[...] = h
    f_ref[...] = _rms(h, pre_ref[...]).astype(BF16)


def _mix_out(mixed, x2, w_out, post_w, pre_w, tm):
    t, d = x2.shape
    row = pl.BlockSpec((tm, d), lambda m: (m, 0))
    vec = pl.BlockSpec((1, d), lambda m: (0, 0))
    return pl.pallas_call(
        _mix_out_kernel,
        grid=(t // tm,),
        in_specs=[row, row, pl.BlockSpec((d, d), lambda m: (0, 0)), vec, vec],
        out_specs=[row, row],
        out_shape=[jax.ShapeDtypeStruct((t, d), F32), jax.ShapeDtypeStruct((t, d), BF16)],
        compiler_params=_params("parallel"),
        name="mix_out",
    )(mixed, x2, w_out, post_w, pre_w)


def _ffn_kernel(f_ref, wg_ref, wu_ref, wd_ref, o_ref, acc_scr):
    kf = pl.program_id(1)

    @pl.when(kf == 0)
    def _():
        acc_scr[...] = jnp.zeros(acc_scr.shape, F32)

    f = f_ref[...]
    g = _dot(f, wg_ref[...])
    u = _dot(f, wu_ref[...])
    a = (g * _sigmoid(g) * u).astype(BF16)
    acc_scr[...] += _dot(a, wd_ref[...])

    @pl.when(kf == pl.num_programs(1) - 1)
    def _():
        o_ref[...] = acc_scr[...].astype(BF16)


def _ffn(f, wg, wu, wd, tm, tf):
    t, d = f.shape
    dff = wg.shape[1]
    return pl.pallas_call(
        _ffn_kernel,
        grid=(t // tm, dff // tf),
        in_specs=[
            pl.BlockSpec((tm, d), lambda m, k: (m, 0)),
            pl.BlockSpec((d, tf), lambda m, k: (0, k)),
            pl.BlockSpec((d, tf), lambda m, k: (0, k)),
            pl.BlockSpec((tf, d), lambda m, k: (k, 0)),
        ],
        out_specs=pl.BlockSpec((tm, d), lambda m, k: (m, 0)),
        out_shape=jax.ShapeDtypeStruct((t, d), BF16),
        scratch_shapes=[pltpu.VMEM((tm, d), F32)],
        compiler_params=_params("parallel", "arbitrary"),
        name="ffn",
    )(f, wg, wu, wd)


def _ple_kernel(h_ref, d_ref, p_ref, fpost_ref, ppre_ref, ppost_ref, wg_ref, wp_ref, o_ref):
    h = h_ref[...] + _rms(d_ref[...].astype(F32), fpost_ref[...])
    hn = _rms(h, ppre_ref[...]).astype(BF16)
    gate = _sigmoid(_dot(hn, wg_ref[...]))
    e = _dot(p_ref[...].astype(BF16), wp_ref[...]) * gate
    o_ref[...] = h + _rms(e, ppost_ref[...])


def _ple(h1, dff, p2, fpost, ppre, ppost, wg, wp, tm):
    t, d = h1.shape
    row = pl.BlockSpec((tm, d), lambda m: (m, 0))
    vec = pl.BlockSpec((1, d), lambda m: (0, 0))
    return pl.pallas_call(
        _ple_kernel,
        grid=(t // tm,),
        in_specs=[row, row, pl.BlockSpec((tm, p2.shape[1]), lambda m: (m, 0)), vec, vec, vec,
                  pl.BlockSpec((d, d), lambda m: (0, 0)),
                  pl.BlockSpec((p2.shape[1], d), lambda m: (0, 0))],
        out_specs=row,
        out_shape=jax.ShapeDtypeStruct((t, d), F32),
        compiler_params=_params("parallel"),
        name="ple",
    )(h1, dff, p2, fpost, ppre, ppost, wg, wp)


def _expand_matrix():
    e = np.zeros((128, 2 * GROUP_WIDTH), np.float32)
    for part in range(3):
        for r in range(HEADS_PER_GROUP):
            lo = r * SSM_HEADDIM
            e[part * HEADS_PER_GROUP + r, lo:lo + SSM_HEADDIM] = 1.0
            e[(3 + part) * HEADS_PER_GROUP + r, GROUP_WIDTH + lo:GROUP_WIDTH + lo + SSM_HEADDIM] = 1.0
    return e


def _layer(h, p_i, pos, invf, mix_norm_pre, mix_norm_post, w_in, q_norm, w_uq, kv_norm, w_ukv,
           conv_w, conv_b, dt_bias, a_log, d_skip, ssm_norm, w_attn_o, w_ssm_o, w_out,
           ffn_norm_pre, ffn_norm_post, w_gate, w_up, w_down,
           ple_norm_pre, ple_norm_post, w_ple_gate, w_ple):
    batch, seq, d = h.shape
    t = batch * seq
    d_inner = SSM_GROUPS * GROUP_WIDTH
    bc_width = SSM_GROUPS * D_STATE
    n_ssm_heads = SSM_GROUPS * HEADS_PER_GROUP
    vec = lambda v: v.reshape(1, -1).astype(F32)

    o_kr = Q_LORA + KV_LORA
    o_z = o_kr + QK_ROPE
    o_xbc = o_z + d_inner
    o_dt = o_xbc + d_inner + 2 * bc_width
    o_ga = o_dt + n_ssm_heads
    w_main = jnp.concatenate([w_in[:, :o_kr], w_in[:, o_z:o_dt], w_in[:, o_ga:]], axis=1).astype(BF16)
    half = QK_ROPE // 2
    w_small_t = jnp.concatenate([w_in[:, o_kr:o_z], w_in[:, o_kr + half:o_z], w_in[:, o_kr:o_kr + half],
                                 w_in[:, o_dt:o_ga]], axis=1).T.astype(BF16)
    dt_blk = 2 * QK_ROPE // HEADS_PER_GROUP
    z_blk = (Q_LORA + KV_LORA) // GROUP_WIDTH
    x_blk = z_blk + SSM_GROUPS
    b_blk = (Q_LORA + KV_LORA + 2 * d_inner) // D_STATE
    c_blk = b_blk + SSM_GROUPS
    g_off = Q_LORA + KV_LORA + 2 * d_inner + 2 * bc_width

    x2 = h.reshape(t, d)
    proj, small_t = _inproj(x2, vec(mix_norm_pre), w_main, w_small_t, tm=512, tn=1024)

    wq = w_uq.reshape(Q_LORA, N_HEADS_MLA, QK_DIM)
    wq = jnp.concatenate([wq, wq[:, :, QK_NOPE + half:], wq[:, :, QK_NOPE:QK_NOPE + half]], axis=2)
    wq_t = jnp.transpose(wq, (1, 2, 0)).astype(BF16)
    wkv = w_ukv.reshape(KV_LORA, N_HEADS_MLA, QK_NOPE + V_DIM)
    wk = jnp.transpose(wkv[:, :, :QK_NOPE], (1, 0, 2)).astype(BF16)
    wv_t = jnp.transpose(wkv[:, :, QK_NOPE:], (1, 2, 0)).astype(BF16)
    qt, k, vt = _mla_prep(proj, small_t, pos, invf, vec(q_norm), vec(kv_norm), wq_t, wk, wv_t,
                          batch, seq, tm=512)
    attn = _attention(qt, k, vt, tq=1024).reshape(t, N_HEADS_MLA * V_DIM)

    expand = jnp.asarray(_expand_matrix(), BF16)
    tri = jnp.asarray(np.triu(np.ones((CHUNK, CHUNK), np.float32)), BF16)
    dskip_e = jnp.repeat(d_skip.astype(F32), SSM_HEADDIM).reshape(1, d_inner)
    ssm = _ssd(proj, small_t, conv_w.astype(F32), vec(conv_b), dt_bias.reshape(-1, 1).astype(F32),
               a_log.reshape(-1, 1).astype(F32), dskip_e, vec(ssm_norm), expand, tri, batch, seq,
               (z_blk, x_blk, b_blk, c_blk, dt_blk))

    tn_mix = 512
    mixed = _mix(attn, ssm, proj, w_attn_o.astype(BF16), w_ssm_o.astype(BF16),
                 g_off // tn_mix, (g_off + d) // tn_mix, tm=1024, tn=tn_mix)
    h1, f = _mix_out(mixed, x2, w_out.astype(BF16), vec(mix_norm_post), vec(ffn_norm_pre), tm=512)
    dff = _ffn(f, w_gate.astype(BF16), w_up.astype(BF16), w_down.astype(BF16), tm=1024, tf=512)
    out = _ple(h1, dff, p_i.reshape(t, -1), vec(ffn_norm_post), vec(ple_norm_pre), vec(ple_norm_post),
               w_ple_gate.astype(BF16), w_ple.astype(BF16), tm=512)
    return out.reshape(batch, seq, d)


def kernel(x, p, positions, mix_norm_pre, mix_norm_post, w_in, q_norm, w_uq, kv_norm, w_ukv, conv_w, conv_b, dt_bias, a_log, d_skip, ssm_norm, w_attn_o, w_ssm_o, w_out, ffn_norm_pre, ffn_norm_post, w_gate, w_up, w_down, ple_norm_pre, ple_norm_post, w_ple_gate, w_ple):
    invf = ROPE_THETA ** (-jnp.arange(0, QK_ROPE, 2, dtype=F32) / QK_ROPE)
    invf = jnp.tile(invf, 2 * QK_ROPE // invf.shape[0]).reshape(2 * QK_ROPE, 1)
    pos = positions.reshape(1, -1)
    h = x
    for i in range(w_in.shape[0]):
        h = _layer(h, p[i], pos, invf, mix_norm_pre[i], mix_norm_post[i], w_in[i], q_norm[i], w_uq[i],
                   kv_norm[i], w_ukv[i], conv_w[i], conv_b[i], dt_bias[i], a_log[i], d_skip[i], ssm_norm[i],
                   w_attn_o[i], w_ssm_o[i], w_out[i], ffn_norm_pre[i], ffn_norm_post[i],
                   w_gate[i], w_up[i], w_down[i], ple_norm_pre[i], ple_norm_post[i], w_ple_gate[i], w_ple[i])
    return h
```

```python
import functools

import jax
import jax.numpy as jnp
import numpy as np
from jax import lax
from jax.experimental import pallas as pl
from jax.experimental.pallas import tpu as pltpu

F32 = jnp.float32
BF16 = jnp.bfloat16

EPS = 1e-6
ROPE_THETA = 10000.0
LOG2_E = 1.4426950408889634

N_HEADS_MLA = 16
Q_LORA = 512
KV_LORA = 512
QK_NOPE = 128
QK_ROPE = 64
V_DIM = 128
QK_DIM = QK_NOPE + QK_ROPE
QK_PAD = 256

SSM_HEADDIM = 64
SSM_GROUPS = 8
HEADS_PER_GROUP = 8
GROUP_WIDTH = HEADS_PER_GROUP * SSM_HEADDIM
D_STATE = 128
CONV_WIDTH = 4
CHUNK = 256
CONV_TAIL = 8

VMEM_LIMIT = 56 * 1024 * 1024


def _params(*sem):
    return pltpu.CompilerParams(dimension_semantics=sem, vmem_limit_bytes=VMEM_LIMIT)


def _rms(x, w):
    return x * lax.rsqrt(jnp.mean(x * x, axis=-1, keepdims=True) + EPS) * w


def _sigmoid(x):
    return 0.5 + 0.5 * jnp.tanh(0.5 * x)


def _silu(x):
    h = 0.5 * x
    return h + h * jnp.tanh(h)


def _dot(a, b):
    return jnp.dot(a, b, preferred_element_type=F32)


def _dot_nt(a, b):
    return lax.dot_general(a, b, (((1,), (1,)), ((), ())), preferred_element_type=F32)


def _inproj_kernel(x_ref, nw_ref, *refs, bounds):
    nseg = len(bounds) - 1
    w_refs, (wt_ref, o_ref, ot_ref, u_scr) = refs[:nseg], refs[nseg:]
    n = pl.program_id(1)

    @pl.when(n == 0)
    def _():
        u = _rms(x_ref[...], nw_ref[...]).astype(BF16)
        u_scr[...] = u
        ot_ref[...] = _dot_nt(wt_ref[...], u)

    for seg in range(nseg):
        @pl.when((n >= bounds[seg]) & (n < bounds[seg + 1]))
        def _(w_ref=w_refs[seg]):
            o_ref[...] = _dot(u_scr[...], w_ref[...]).astype(BF16)


def _inproj(x2, norm_w, w_segs, w_small_t, tm, tn):
    t, d = x2.shape
    bounds = [0]
    for w in w_segs:
        bounds.append(bounds[-1] + w.shape[1] // tn)
    n_main = bounds[-1] * tn
    n_small = w_small_t.shape[0]

    def seg_spec(lo, hi):
        return pl.BlockSpec((d, tn), lambda m, n: (0, jnp.clip(n - lo, 0, hi - lo - 1)))

    return pl.pallas_call(
        functools.partial(_inproj_kernel, bounds=tuple(bounds)),
        grid=(t // tm, n_main // tn),
        in_specs=[
            pl.BlockSpec((tm, d), lambda m, n: (m, 0)),
            pl.BlockSpec((1, d), lambda m, n: (0, 0)),
            *[seg_spec(bounds[i], bounds[i + 1]) for i in range(len(w_segs))],
            pl.BlockSpec((n_small, d), lambda m, n: (0, 0)),
        ],
        out_specs=[
            pl.BlockSpec((tm, tn), lambda m, n: (m, n)),
            pl.BlockSpec((n_small, tm), lambda m, n: (0, m)),
        ],
        out_shape=[
            jax.ShapeDtypeStruct((t, n_main), BF16),
            jax.ShapeDtypeStruct((n_small, t), F32),
        ],
        scratch_shapes=[pltpu.VMEM((tm, d), BF16)],
        compiler_params=_params("parallel", "arbitrary"),
        name="inproj",
    )(x2, norm_w, *w_segs, w_small_t)


def _mla_prep_kernel(cq_ref, ckv_ref, kr_ref, pos_ref, invf_ref, qn_ref, kvn_ref, wq_ref, wk_ref, wv_ref,
                     qt_ref, k_ref, vt_ref, cqn_scr, ckvn_scr, tab_scr, kpe_scr):
    def rope_t(pair):
        t = pair * tab_scr[...]
        return t[:QK_ROPE] + t[QK_ROPE:]

    @pl.when(pl.program_id(1) == 0)
    def _():
        cqn_scr[...] = _rms(cq_ref[...].astype(F32), qn_ref[...]).astype(BF16)
        ckvn_scr[...] = _rms(ckv_ref[...].astype(F32), kvn_ref[...]).astype(BF16)
        ang = invf_ref[...] * pos_ref[...].astype(F32)
        row = lax.broadcasted_iota(jnp.int32, ang.shape, 0)
        sin = jnp.sin(ang)
        neg = (row >= QK_ROPE) & (row < QK_ROPE + QK_ROPE // 2)
        tab_scr[...] = jnp.where(row < QK_ROPE, jnp.cos(ang), jnp.where(neg, -sin, sin))
        kpe_t = rope_t(kr_ref[...])
        kpe_scr[...] = jnp.concatenate([kpe_t, jnp.zeros_like(kpe_t)], axis=0).T.astype(BF16)

    scale = QK_DIM ** -0.5 * LOG2_E
    ckvn = ckvn_scr[...]
    qo = _dot_nt(wq_ref[...], cqn_scr[...])
    kn = _dot(ckvn, wk_ref[...]).astype(BF16)
    vo = _dot_nt(wv_ref[...], ckvn).astype(BF16)
    for hh in range(qt_ref.shape[1]):
        qh = qo[hh * QK_PAD:(hh + 1) * QK_PAD]
        qt_ref[0, hh, :QK_NOPE, :] = (qh[:QK_NOPE] * scale).astype(BF16)
        q_pe = (rope_t(qh[QK_NOPE:]) * scale).astype(BF16)
        qt_ref[0, hh, QK_NOPE:QK_DIM, :] = q_pe
        qt_ref[0, hh, QK_DIM:, :] = jnp.zeros_like(q_pe)
        k_ref[0, hh, :, :QK_NOPE] = kn[:, hh * QK_NOPE:(hh + 1) * QK_NOPE]
        k_ref[0, hh, :, QK_NOPE:] = kpe_scr[...]
        vt_ref[0, hh] = vo[hh * V_DIM:(hh + 1) * V_DIM]


def _mla_prep(proj, small_t, pos, invf, q_norm, kv_norm, wq_t, wk, wv_t, batch, seq, tm, hb):
    t = proj.shape[0]
    spb = seq // tm
    h = N_HEADS_MLA
    return pl.pallas_call(
        _mla_prep_kernel,
        grid=(t // tm, h // hb),
        in_specs=[
            pl.BlockSpec((tm, Q_LORA), lambda m, hh: (m, 0)),
            pl.BlockSpec((tm, KV_LORA), lambda m, hh: (m, 1)),
            pl.BlockSpec((2 * QK_ROPE, tm), lambda m, hh: (0, m)),
            pl.BlockSpec((1, tm), lambda m, hh: (0, m)),
            pl.BlockSpec((2 * QK_ROPE, 1), lambda m, hh: (0, 0)),
            pl.BlockSpec((1, Q_LORA), lambda m, hh: (0, 0)),
            pl.BlockSpec((1, KV_LORA), lambda m, hh: (0, 0)),
            pl.BlockSpec((hb * QK_PAD, Q_LORA), lambda m, hh: (hh, 0)),
            pl.BlockSpec((KV_LORA, hb * QK_NOPE), lambda m, hh: (0, hh)),
            pl.BlockSpec((hb * V_DIM, KV_LORA), lambda m, hh: (hh, 0)),
        ],
        out_specs=[
            pl.BlockSpec((1, hb, QK_PAD, tm), lambda m, hh: (m // spb, hh, 0, m % spb)),
            pl.BlockSpec((1, hb, tm, QK_PAD), lambda m, hh: (m // spb, hh, m % spb, 0)),
            pl.BlockSpec((1, hb, V_DIM, tm), lambda m, hh: (m // spb, hh, 0, m % spb)),
        ],
        out_shape=[
            jax.ShapeDtypeStruct((batch, h, QK_PAD, seq), BF16),
            jax.ShapeDtypeStruct((batch, h, seq, QK_PAD), BF16),
            jax.ShapeDtypeStruct((batch, h, V_DIM, seq), BF16),
        ],
        scratch_shapes=[
            pltpu.VMEM((tm, Q_LORA), BF16),
            pltpu.VMEM((tm, KV_LORA), BF16),
            pltpu.VMEM((2 * QK_ROPE, tm), F32),
            pltpu.VMEM((tm, 2 * QK_ROPE), BF16),
        ],
        compiler_params=_params("parallel", "arbitrary"),
        name="mla_prep",
    )(proj, proj, small_t, pos, invf, q_norm, kv_norm, wq_t, wk, wv_t)


def _attn_kernel(qt_ref, k_ref, vt_ref, o_ref, m_scr, l_scr, acc_scr, sa_scr, sb_scr, *, tq):
    qi = pl.program_id(2)
    m_scr[...] = jnp.full(m_scr.shape, -jnp.inf, F32)
    l_scr[...] = jnp.zeros(l_scr.shape, F32)
    acc_scr[...] = jnp.zeros(acc_scr.shape, F32)

    def scores(ki):
        start = pl.multiple_of(ki * tq, tq)
        return _dot(k_ref[0, 0, pl.ds(start, tq), :], qt_ref[0, 0])

    def accumulate(ki, s):
        start = pl.multiple_of(ki * tq, tq)
        m_prev = m_scr[...]
        m_new = jnp.maximum(m_prev, jnp.max(s, axis=0, keepdims=True))
        alpha = jnp.exp2(m_prev - m_new)
        p = jnp.exp2(s - m_new)
        l_scr[...] = alpha * l_scr[...] + jnp.sum(p, axis=0, keepdims=True)
        acc_scr[...] = alpha * acc_scr[...] + _dot(vt_ref[0, 0, :, pl.ds(start, tq)], p.astype(BF16))
        m_scr[...] = m_new

    def accumulate_diagonal(s):
        key = lax.broadcasted_iota(jnp.int32, s.shape, 0)
        qry = lax.broadcasted_iota(jnp.int32, s.shape, 1)
        accumulate(qi, jnp.where(key <= qry, s, -jnp.inf))

    sa_scr[...] = scores(0)

    def pair(j, carry):
        sb_scr[...] = scores(2 * j + 1)
        accumulate(2 * j, sa_scr[...])
        sa_scr[...] = scores(2 * j + 2)
        accumulate(2 * j + 1, sb_scr[...])
        return carry

    lax.fori_loop(0, qi // 2, pair, 0)

    @pl.when(qi % 2 == 0)
    def _():
        accumulate_diagonal(sa_scr[...])

    @pl.when(qi % 2 == 1)
    def _():
        sb_scr[...] = scores(qi)
        accumulate(qi - 1, sa_scr[...])
        accumulate_diagonal(sb_scr[...])

    o_ref[0] = (acc_scr[...] / l_scr[...]).T.astype(BF16)


def _attention(qt, k, vt, tq):
    b, h, s, _ = k.shape
    return pl.pallas_call(
        functools.partial(_attn_kernel, tq=tq),
        grid=(b, h, s // tq),
        in_specs=[
            pl.BlockSpec((1, 1, QK_PAD, tq), lambda bb, hh, qi: (bb, hh, 0, qi)),
            pl.BlockSpec((1, 1, s, QK_PAD), lambda bb, hh, qi: (bb, hh, 0, 0)),
            pl.BlockSpec((1, 1, V_DIM, s), lambda bb, hh, qi: (bb, hh, 0, 0)),
        ],
        out_specs=pl.BlockSpec((1, tq, V_DIM), lambda bb, hh, qi: (bb, qi, hh)),
        out_shape=jax.ShapeDtypeStruct((b, s, h * V_DIM), BF16),
        scratch_shapes=[
            pltpu.VMEM((1, tq), F32),
            pltpu.VMEM((1, tq), F32),
            pltpu.VMEM((V_DIM, tq), F32),
            pltpu.VMEM((tq, tq), F32),
            pltpu.VMEM((tq, tq), F32),
        ],
        compiler_params=_params("parallel", "parallel", "arbitrary"),
        name="attention",
    )(qt, k, vt)


def _split3(x):
    hi = x.astype(BF16).astype(F32)
    r = x - hi
    mid = r.astype(BF16).astype(F32)
    return hi, mid, r - mid


def _ssd_kernel(x_ref, b_ref, c_ref, z_ref, dt_ref, cwx_ref, cwb_ref, cwc_ref, cbx_ref, cbb_ref, cbc_ref,
                dtb_ref, alog_ref, dskip_ref, nw_ref, expand_ref, tri_ref, shift_ref, o_ref,
                state_scr, tail_scr):
    L = x_ref.shape[0]

    @pl.when(pl.program_id(2) == 0)
    def _():
        state_scr[...] = jnp.zeros(state_scr.shape, F32)
        tail_scr[...] = jnp.zeros(tail_scr.shape, F32)

    raw = jnp.concatenate([x_ref[...], b_ref[...], c_ref[...]], axis=1)
    cw = jnp.concatenate([cwx_ref[...], cwb_ref[...], cwc_ref[...]], axis=1)
    bias = jnp.concatenate([cbx_ref[...], cbb_ref[...], cbc_ref[...]], axis=1)
    shifted = _dot(shift_ref[...], raw)
    rawf = raw.astype(F32)
    conv = bias + cw[CONV_WIDTH - 1:CONV_WIDTH, :] * rawf
    for k in range(1, CONV_WIDTH):
        conv = conv + cw[CONV_WIDTH - 1 - k:CONV_WIDTH - k, :] * shifted[(k - 1) * L:k * L]
    ext = jnp.concatenate([tail_scr[...], rawf[:CONV_TAIL]], axis=0)
    head = bias
    for tap in range(CONV_WIDTH):
        off = CONV_TAIL - (CONV_WIDTH - 1) + tap
        head = head + cw[tap:tap + 1, :] * ext[off:off + CONV_TAIL]
    conv = jnp.concatenate([head, conv[CONV_TAIL:]], axis=0)
    tail_scr[...] = rawf[L - CONV_TAIL:, :]
    act = _silu(conv)
    xc = act[:, :GROUP_WIDTH]
    bc = act[:, GROUP_WIDTH:GROUP_WIDTH + D_STATE]
    cc = act[:, GROUP_WIDTH + D_STATE:]

    dtr = dt_ref[...] + dtb_ref[...]
    dt = jnp.maximum(dtr, 0.0) + jnp.log1p(jnp.exp(-jnp.abs(dtr)))
    da = dt * (-jnp.exp(alog_ref[...])) * LOG2_E
    d_hi, d_mid, d_lo = _split3(da)
    tri = tri_ref[...]
    cum = _dot(d_hi.astype(BF16), tri) + _dot(d_mid.astype(BF16), tri) + _dot(d_lo.astype(BF16), tri)

    t_hi, t_mid, t_lo = _split3(dt)
    c_hi, c_mid, c_lo = _split3(cum)
    stacked = jnp.concatenate(
        [t_hi, t_mid, t_lo, c_hi, c_mid, c_lo, cum, jnp.zeros((128 - 7 * HEADS_PER_GROUP, L), F32)], axis=0)
    cols = stacked.T
    expanded = _dot(cols.astype(BF16), expand_ref[...])
    dt_e = expanded[:, :GROUP_WIDTH]
    cum_e = expanded[:, GROUP_WIDTH:]

    xdt = xc * dt_e
    xdt_b = xdt.astype(BF16)
    bc_b = bc.astype(BF16)
    cc_b = cc.astype(BF16)
    last_e = cum_e[L - 1:L, :]
    state = state_scr[...]

    y = _dot(cc_b, state.astype(BF16)) * jnp.exp2(cum_e)
    y = y + dskip_ref[...] * xc

    w_end = jnp.exp2(last_e - cum_e)
    state_scr[...] = state * jnp.exp2(last_e) + _dot(bc.T.astype(BF16), (xdt * w_end).astype(BF16))

    cb = _dot_nt(cc_b, bc_b)
    ri = lax.broadcasted_iota(jnp.int32, (L, L), 0)
    ci = lax.broadcasted_iota(jnp.int32, (L, L), 1)
    causal = ci <= ri
    lane = lax.broadcasted_iota(jnp.int32, (L, 128), 1)
    pairs = []
    for pr in range(HEADS_PER_GROUP // 2):
        xp = xdt_b[:, pr * 128:(pr + 1) * 128]
        acc = None
        for sub in range(2):
            r = 2 * pr + sub
            col = cols[:, 6 * HEADS_PER_GROUP + r:6 * HEADS_PER_GROUP + r + 1]
            seg = col - cum[r:r + 1, :]
            m = (cb * jnp.exp2(jnp.where(causal, seg, -jnp.inf))).astype(BF16)
            keep = (lane < SSM_HEADDIM) if sub == 0 else (lane >= SSM_HEADDIM)
            part = _dot(m, jnp.where(keep, xp, jnp.zeros_like(xp)))
            acc = part if acc is None else acc + part
        pairs.append(acc)
    y = y + jnp.concatenate(pairs, axis=1)

    y = y * _silu(z_ref[...].astype(F32))
    o_ref[...] = _rms(y, nw_ref[...]).astype(BF16)


def _shift_matrices():
    L = CHUNK
    shift = np.zeros(((CONV_WIDTH - 1) * L, L), np.float32)
    for k in range(1, CONV_WIDTH):
        for t in range(k, L):
            shift[(k - 1) * L + t, t - k] = 1.0
    return shift


def _ssd(proj, dt_t, conv_w, conv_b, dt_bias, a_log, dskip_e, ssm_norm, expand, tri, shift,
         batch, seq, offs):
    t = proj.shape[0]
    L = CHUNK
    nc = seq // L
    g = SSM_GROUPS
    d_inner = g * GROUP_WIDTH
    row = lambda bb, gg, c: bb * nc + c
    z_blk, x_blk, b_blk, c_blk, dt_blk = offs
    return pl.pallas_call(
        _ssd_kernel,
        grid=(batch, g, nc),
        in_specs=[
            pl.BlockSpec((L, GROUP_WIDTH), lambda bb, gg, c: (row(bb, gg, c), x_blk + gg)),
            pl.BlockSpec((L, D_STATE), lambda bb, gg, c: (row(bb, gg, c), b_blk + gg)),
            pl.BlockSpec((L, D_STATE), lambda bb, gg, c: (row(bb, gg, c), c_blk + gg)),
            pl.BlockSpec((L, GROUP_WIDTH), lambda bb, gg, c: (row(bb, gg, c), z_blk + gg)),
            pl.BlockSpec((HEADS_PER_GROUP, L), lambda bb, gg, c: (dt_blk + gg, row(bb, gg, c))),
            pl.BlockSpec((CONV_WIDTH, GROUP_WIDTH), lambda bb, gg, c: (0, gg)),
            pl.BlockSpec((CONV_WIDTH, D_STATE), lambda bb, gg, c: (0, d_inner // D_STATE + gg)),
            pl.BlockSpec((CONV_WIDTH, D_STATE), lambda bb, gg, c: (0, d_inner // D_STATE + g + gg)),
            pl.BlockSpec((1, GROUP_WIDTH), lambda bb, gg, c: (0, gg)),
            pl.BlockSpec((1, D_STATE), lambda bb, gg, c: (0, d_inner // D_STATE + gg)),
            pl.BlockSpec((1, D_STATE), lambda bb, gg, c: (0, d_inner // D_STATE + g + gg)),
            pl.BlockSpec((HEADS_PER_GROUP, 1), lambda bb, gg, c: (gg, 0)),
            pl.BlockSpec((HEADS_PER_GROUP, 1), lambda bb, gg, c: (gg, 0)),
            pl.BlockSpec((1, GROUP_WIDTH), lambda bb, gg, c: (0, gg)),
            pl.BlockSpec((1, GROUP_WIDTH), lambda bb, gg, c: (0, gg)),
            pl.BlockSpec((128, 2 * GROUP_WIDTH), lambda bb, gg, c: (0, 0)),
            pl.BlockSpec((L, L), lambda bb, gg, c: (0, 0)),
            pl.BlockSpec(((CONV_WIDTH - 1) * L, L), lambda bb, gg, c: (0, 0)),
        ],
        out_specs=pl.BlockSpec((L, GROUP_WIDTH), lambda bb, gg, c: (row(bb, gg, c), gg)),
        out_shape=jax.ShapeDtypeStruct((t, d_inner), BF16),
        scratch_shapes=[
            pltpu.VMEM((D_STATE, GROUP_WIDTH), F32),
            pltpu.VMEM((CONV_TAIL, GROUP_WIDTH + 2 * D_STATE), F32),
        ],
        compiler_params=_params("parallel", "parallel", "arbitrary"),
        name="ssd",
    )(proj, proj, proj, proj, dt_t, conv_w, conv_w, conv_w, conv_b, conv_b, conv_b,
      dt_bias, a_log, dskip_e, ssm_norm, expand, tri, shift)


def _mix_kernel(attn_ref, ssm_ref, ga_ref, gs_ref, wa_ref, ws_ref, o_ref):
    a = _dot(attn_ref[...], wa_ref[...])
    s = _dot(ssm_ref[...], ws_ref[...])
    mixed = _sigmoid(ga_ref[...].astype(F32)) * a + _sigmoid(gs_ref[...].astype(F32)) * s
    o_ref[...] = mixed.astype(BF16)


def _mix(attn, ssm, proj, wa, ws, ga_blk, gs_blk, tm, tn):
    t = attn.shape[0]
    d = wa.shape[1]
    return pl.pallas_call(
        _mix_kernel,
        grid=(t // tm, d // tn),
        in_specs=[
            pl.BlockSpec((tm, attn.shape[1]), lambda m, n: (m, 0)),
            pl.BlockSpec((tm, ssm.shape[1]), lambda m, n: (m, 0)),
            pl.BlockSpec((tm, tn), lambda m, n: (m, ga_blk + n)),
            pl.BlockSpec((tm, tn), lambda m, n: (m, gs_blk + n)),
            pl.BlockSpec((wa.shape[0], tn), lambda m, n: (0, n)),
            pl.BlockSpec((ws.shape[0], tn), lambda m, n: (0, n)),
        ],
        out_specs=pl.BlockSpec((tm, tn), lambda m, n: (m, n)),
        out_shape=jax.ShapeDtypeStruct((t, d), BF16),
        compiler_params=_params("parallel", "arbitrary"),
        name="mix",
    )(attn, ssm, proj, proj, wa, ws)


def _mix_out_kernel(mixed_ref, x_ref, w_ref, post_ref, pre_ref, h_ref, f_ref):
    o = _dot(mixed_ref[...], w_ref[...])
    h = x_ref[...] + _rms(o, post_ref[...])
    h_ref[...] = h
    f_ref[...] = _rms(h, pre_ref[...]).astype(BF16)


def _mix_out(mixed, x2, w_out, post_w, pre_w, tm):
    t, d = x2.shape
    row = pl.BlockSpec((tm, d), lambda m: (m, 0))
    vec = pl.BlockSpec((1, d), lambda m: (0, 0))
    return pl.pallas_call(
        _mix_out_kernel,
        grid=(t // tm,),
        in_specs=[row, row, pl.BlockSpec((d, d), lambda m: (0, 0)), vec, vec],
        out_specs=[row, row],
        out_shape=[jax.ShapeDtypeStruct((t, d), F32), jax.ShapeDtypeStruct((t, d), BF16)],
        compiler_params=_params("parallel"),
        name="mix_out",
    )(mixed, x2, w_out, post_w, pre_w)


def _ffn_kernel(f_ref, wg_ref, wu_ref, wd_ref, o_ref, acc_scr):
    kf = pl.program_id(1)

    @pl.when(kf == 0)
    def _():
        acc_scr[...] = jnp.zeros(acc_scr.shape, F32)

    f = f_ref[...]
    g = _dot(f, wg_ref[...])
    u = _dot(f, wu_ref[...])
    a = (_silu(g) * u).astype(BF16)
    acc_scr[...] += _dot(a, wd_ref[...])

    @pl.when(kf == pl.num_programs(1) - 1)
    def _():
        o_ref[...] = acc_scr[...].astype(BF16)


def _ffn(f, wg, wu, wd, tm, tf):
    t, d = f.shape
    dff = wg.shape[1]
    return pl.pallas_call(
        _ffn_kernel,
        grid=(t // tm, dff // tf),
        in_specs=[
            pl.BlockSpec((tm, d), lambda m, k: (m, 0)),
            pl.BlockSpec((d, tf), lambda m, k: (0, k)),
            pl.BlockSpec((d, tf), lambda m, k: (0, k)),
            pl.BlockSpec((tf, d), lambda m, k: (k, 0)),
        ],
        out_specs=pl.BlockSpec((tm, d), lambda m, k: (m, 0)),
        out_shape=jax.ShapeDtypeStruct((t, d), BF16),
        scratch_shapes=[pltpu.VMEM((tm, d), F32)],
        compiler_params=_params("parallel", "arbitrary"),
        name="ffn",
    )(f, wg, wu, wd)


def _ple_kernel(h_ref, d_ref, p_ref, fpost_ref, ppre_ref, ppost_ref, wg_ref, wp_ref, o_ref):
    h = h_ref[...] + _rms(d_ref[...].astype(F32), fpost_ref[...])
    hn = _rms(h, ppre_ref[...]).astype(BF16)
    gate = _sigmoid(_dot(hn, wg_ref[...]))
    e = _dot(p_ref[...].astype(BF16), wp_ref[...]) * gate
    o_ref[...] = h + _rms(e, ppost_ref[...])


def _ple(h1, dff, p2, fpost, ppre, ppost, wg, wp, tm):
    t, d = h1.shape
    row = pl.BlockSpec((tm, d), lambda m: (m, 0))
    vec = pl.BlockSpec((1, d), lambda m: (0, 0))
    return pl.pallas_call(
        _ple_kernel,
        grid=(t // tm,),
        in_specs=[row, row, pl.BlockSpec((tm, p2.shape[1]), lambda m: (m, 0)), vec, vec, vec,
                  pl.BlockSpec((d, d), lambda m: (0, 0)),
                  pl.BlockSpec((p2.shape[1], d), lambda m: (0, 0))],
        out_specs=row,
        out_shape=jax.ShapeDtypeStruct((t, d), F32),
        compiler_params=_params("parallel"),
        name="ple",
    )(h1, dff, p2, fpost, ppre, ppost, wg, wp)


def _expand_matrix():
    e = np.zeros((128, 2 * GROUP_WIDTH), np.float32)
    for part in range(3):
        for r in range(HEADS_PER_GROUP):
            lo = r * SSM_HEADDIM
            e[part * HEADS_PER_GROUP + r, lo:lo + SSM_HEADDIM] = 1.0
            e[(3 + part) * HEADS_PER_GROUP + r, GROUP_WIDTH + lo:GROUP_WIDTH + lo + SSM_HEADDIM] = 1.0
    return e


def _layer(h, p_i, pos, invf, mix_norm_pre, mix_norm_post, w_in, q_norm, w_uq, kv_norm, w_ukv,
           conv_w, conv_b, dt_bias, a_log, d_skip, ssm_norm, w_attn_o, w_ssm_o, w_out,
           ffn_norm_pre, ffn_norm_post, w_gate, w_up, w_down,
           ple_norm_pre, ple_norm_post, w_ple_gate, w_ple):
    batch, seq, d = h.shape
    t = batch * seq
    d_inner = SSM_GROUPS * GROUP_WIDTH
    bc_width = SSM_GROUPS * D_STATE
    n_ssm_heads = SSM_GROUPS * HEADS_PER_GROUP
    vec = lambda v: v.reshape(1, -1).astype(F32)

    o_kr = Q_LORA + KV_LORA
    o_z = o_kr + QK_ROPE
    o_xbc = o_z + d_inner
    o_dt = o_xbc + d_inner + 2 * bc_width
    o_ga = o_dt + n_ssm_heads
    w_segs = [w_in[:, :o_kr].astype(BF16), w_in[:, o_z:o_dt].astype(BF16), w_in[:, o_ga:].astype(BF16)]
    half = QK_ROPE // 2
    w_small_t = jnp.concatenate([w_in[:, o_kr:o_z], w_in[:, o_kr + half:o_z], w_in[:, o_kr:o_kr + half],
                                 w_in[:, o_dt:o_ga]], axis=1).T.astype(BF16)
    dt_blk = 2 * QK_ROPE // HEADS_PER_GROUP
    z_blk = (Q_LORA + KV_LORA) // GROUP_WIDTH
    x_blk = z_blk + SSM_GROUPS
    b_blk = (Q_LORA + KV_LORA + 2 * d_inner) // D_STATE
    c_blk = b_blk + SSM_GROUPS
    g_off = Q_LORA + KV_LORA + 2 * d_inner + 2 * bc_width

    x2 = h.reshape(t, d)
    proj, small_t = _inproj(x2, vec(mix_norm_pre), w_segs, w_small_t, tm=1024, tn=1024)

    wq = w_uq.reshape(Q_LORA, N_HEADS_MLA, QK_DIM)
    wq = jnp.concatenate([wq, wq[:, :, QK_NOPE + half:], wq[:, :, QK_NOPE:QK_NOPE + half]], axis=2)
    wq_t = jnp.transpose(wq, (1, 2, 0)).reshape(N_HEADS_MLA * QK_PAD, Q_LORA).astype(BF16)
    wkv = w_ukv.reshape(KV_LORA, N_HEADS_MLA, QK_NOPE + V_DIM)
    wk = wkv[:, :, :QK_NOPE].reshape(KV_LORA, N_HEADS_MLA * QK_NOPE).astype(BF16)
    wv_t = jnp.transpose(wkv[:, :, QK_NOPE:], (1, 2, 0)).reshape(N_HEADS_MLA * V_DIM, KV_LORA).astype(BF16)
    qt, k, vt = _mla_prep(proj, small_t, pos, invf, vec(q_norm), vec(kv_norm), wq_t, wk, wv_t,
                          batch, seq, tm=512, hb=4)
    attn = _attention(qt, k, vt, tq=1024).reshape(t, N_HEADS_MLA * V_DIM)

    expand = jnp.asarray(_expand_matrix(), BF16)
    tri = jnp.asarray(np.triu(np.ones((CHUNK, CHUNK), np.float32)), BF16)
    dskip_e = jnp.repeat(d_skip.astype(F32), SSM_HEADDIM).reshape(1, d_inner)
    ssm = _ssd(proj, small_t, conv_w.astype(F32), vec(conv_b), dt_bias.reshape(-1, 1).astype(F32),
               a_log.reshape(-1, 1).astype(F32), dskip_e, vec(ssm_norm), expand, tri,
               jnp.asarray(_shift_matrices(), BF16), batch, seq,
               (z_blk, x_blk, b_blk, c_blk, dt_blk))

    tn_mix = 512
    mixed = _mix(attn, ssm, proj, w_attn_o.astype(BF16), w_ssm_o.astype(BF16),
                 g_off // tn_mix, (g_off + d) // tn_mix, tm=1024, tn=tn_mix)
    h1, f = _mix_out(mixed, x2, w_out.astype(BF16), vec(mix_norm_post), vec(ffn_norm_pre), tm=512)
    dff = _ffn(f, w_gate.astype(BF16), w_up.astype(BF16), w_down.astype(BF16), tm=1024, tf=512)
    out = _ple(h1, dff, p_i.reshape(t, -1), vec(ffn_norm_post), vec(ple_norm_pre), vec(ple_norm_post),
               w_ple_gate.astype(BF16), w_ple.astype(BF16), tm=512)
    return out.reshape(batch, seq, d)


def kernel(x, p, positions, mix_norm_pre, mix_norm_post, w_in, q_norm, w_uq, kv_norm, w_ukv, conv_w, conv_b, dt_bias, a_log, d_skip, ssm_norm, w_attn_o, w_ssm_o, w_out, ffn_norm_pre, ffn_norm_post, w_gate, w_up, w_down, ple_norm_pre, ple_norm_post, w_ple_gate, w_ple):
    invf = ROPE_THETA ** (-jnp.arange(0, QK_ROPE, 2, dtype=F32) / QK_ROPE)
    invf = jnp.tile(invf, 2 * QK_ROPE // invf.shape[0]).reshape(2 * QK_ROPE, 1)
    pos = positions.reshape(1, -1)
    h = x
    for i in range(w_in.shape[0]):
        h = _layer(h, p[i], pos, invf, mix_norm_pre[i], mix_norm_post[i], w_in[i], q_norm[i], w_uq[i],
                   kv_norm[i], w_ukv[i], conv_w[i], conv_b[i], dt_bias[i], a_log[i], d_skip[i], ssm_norm[i],
                   w_attn_o[i], w_ssm_o[i], w_out[i], ffn_norm_pre[i], ffn_norm_post[i],
                   w_gate[i], w_up[i], w_down[i], ple_norm_pre[i], ple_norm_post[i], w_ple_gate[i], w_ple[i])
    return h
```

```python
import functools

import jax
import jax.numpy as jnp
import numpy as np
from jax import lax
from jax.experimental import pallas as pl
from jax.experimental.pallas import tpu as pltpu

F32 = jnp.float32
BF16 = jnp.bfloat16

EPS = 1e-6
ROPE_THETA = 10000.0
LOG2_E = 1.4426950408889634

N_HEADS_MLA = 16
Q_LORA = 512
KV_LORA = 512
QK_NOPE = 128
QK_ROPE = 64
V_DIM = 128
QK_DIM = QK_NOPE + QK_ROPE
QK_PAD = 256
VT_ROWS = V_DIM + 16

SSM_HEADDIM = 64
SSM_GROUPS = 8
HEADS_PER_GROUP = 8
GROUP_WIDTH = HEADS_PER_GROUP * SSM_HEADDIM
D_STATE = 128
CONV_WIDTH = 4
CHUNK = 256
CONV_TAIL = 8

VMEM_LIMIT = 56 * 1024 * 1024


def _params(*sem):
    return pltpu.CompilerParams(dimension_semantics=sem, vmem_limit_bytes=VMEM_LIMIT)


def _rms(x, w):
    return x * lax.rsqrt(jnp.mean(x * x, axis=-1, keepdims=True) + EPS) * w


def _sigmoid(x):
    return 0.5 + 0.5 * jnp.tanh(0.5 * x)


def _silu(x):
    h = 0.5 * x
    return h + h * jnp.tanh(h)


def _dot(a, b):
    return jnp.dot(a, b, preferred_element_type=F32)


def _dot_nt(a, b):
    return lax.dot_general(a, b, (((1,), (1,)), ((), ())), preferred_element_type=F32)


def _inproj_kernel(x_ref, nw_ref, *refs, bounds):
    nseg = len(bounds) - 1
    w_refs, (wt_ref, o_ref, ot_ref, u_scr) = refs[:nseg], refs[nseg:]
    n = pl.program_id(1)

    @pl.when(n == 0)
    def _():
        u = _rms(x_ref[...], nw_ref[...]).astype(BF16)
        u_scr[...] = u
        ot_ref[...] = _dot_nt(wt_ref[...], u)

    for seg in range(nseg):
        @pl.when((n >= bounds[seg]) & (n < bounds[seg + 1]))
        def _(w_ref=w_refs[seg]):
            o_ref[...] = _dot(u_scr[...], w_ref[...]).astype(BF16)


def _inproj(x2, norm_w, w_segs, w_small_t, tm, tn):
    t, d = x2.shape
    bounds = [0]
    for w in w_segs:
        bounds.append(bounds[-1] + w.shape[1] // tn)
    n_main = bounds[-1] * tn
    n_small = w_small_t.shape[0]

    def seg_spec(lo, hi):
        return pl.BlockSpec((d, tn), lambda m, n: (0, jnp.clip(n - lo, 0, hi - lo - 1)))

    return pl.pallas_call(
        functools.partial(_inproj_kernel, bounds=tuple(bounds)),
        grid=(t // tm, n_main // tn),
        in_specs=[
            pl.BlockSpec((tm, d), lambda m, n: (m, 0)),
            pl.BlockSpec((1, d), lambda m, n: (0, 0)),
            *[seg_spec(bounds[i], bounds[i + 1]) for i in range(len(w_segs))],
            pl.BlockSpec((n_small, d), lambda m, n: (0, 0)),
        ],
        out_specs=[
            pl.BlockSpec((tm, tn), lambda m, n: (m, n)),
            pl.BlockSpec((n_small, tm), lambda m, n: (0, m)),
        ],
        out_shape=[
            jax.ShapeDtypeStruct((t, n_main), BF16),
            jax.ShapeDtypeStruct((n_small, t), F32),
        ],
        scratch_shapes=[pltpu.VMEM((tm, d), BF16)],
        compiler_params=_params("parallel", "arbitrary"),
        name="inproj",
    )(x2, norm_w, *w_segs, w_small_t)


def _mla_prep_kernel(cq_ref, ckv_ref, kr_ref, pos_ref, invf_ref, qn_ref, kvn_ref, wq_ref, wk_ref, wv_ref,
                     qt_ref, k_ref, vt_ref, cqn_scr, ckvn_scr, tab_scr, kpe_scr):
    def rope_t(pair):
        t = pair * tab_scr[...]
        return t[:QK_ROPE] + t[QK_ROPE:]

    @pl.when(pl.program_id(1) == 0)
    def _():
        cqn_scr[...] = _rms(cq_ref[...].astype(F32), qn_ref[...]).astype(BF16)
        ckvn_scr[...] = _rms(ckv_ref[...].astype(F32), kvn_ref[...]).astype(BF16)
        ang = invf_ref[...] * pos_ref[...].astype(F32)
        row = lax.broadcasted_iota(jnp.int32, ang.shape, 0)
        sin = jnp.sin(ang)
        neg = (row >= QK_ROPE) & (row < QK_ROPE + QK_ROPE // 2)
        tab_scr[...] = jnp.where(row < QK_ROPE, jnp.cos(ang), jnp.where(neg, -sin, sin))
        kpe_t = rope_t(kr_ref[...])
        kpe_scr[...] = jnp.concatenate([kpe_t, jnp.zeros_like(kpe_t)], axis=0).T.astype(BF16)

    scale = QK_DIM ** -0.5 * LOG2_E
    ckvn = ckvn_scr[...]
    qo = _dot_nt(wq_ref[...], cqn_scr[...])
    kn = _dot(ckvn, wk_ref[...]).astype(BF16)
    vo = _dot_nt(wv_ref[...], ckvn).astype(BF16)
    pad_row = lax.broadcasted_iota(jnp.int32, (VT_ROWS - V_DIM, vo.shape[1]), 0)
    ones_rows = jnp.where(pad_row == 0, 1.0, 0.0).astype(BF16)
    for hh in range(qt_ref.shape[1]):
        qh = qo[hh * QK_PAD:(hh + 1) * QK_PAD]
        qt_ref[0, hh, :QK_NOPE, :] = (qh[:QK_NOPE] * scale).astype(BF16)
        q_pe = (rope_t(qh[QK_NOPE:]) * scale).astype(BF16)
        qt_ref[0, hh, QK_NOPE:QK_DIM, :] = q_pe
        qt_ref[0, hh, QK_DIM:, :] = jnp.zeros_like(q_pe)
        k_ref[0, hh, :, :QK_NOPE] = kn[:, hh * QK_NOPE:(hh + 1) * QK_NOPE]
        k_ref[0, hh, :, QK_NOPE:] = kpe_scr[...]
        vt_ref[0, hh, :V_DIM, :] = vo[hh * V_DIM:(hh + 1) * V_DIM]
        vt_ref[0, hh, V_DIM:, :] = ones_rows


def _mla_prep(proj, small_t, pos, invf, q_norm, kv_norm, wq_t, wk, wv_t, batch, seq, tm, hb):
    t = proj.shape[0]
    spb = seq // tm
    h = N_HEADS_MLA
    return pl.pallas_call(
        _mla_prep_kernel,
        grid=(t // tm, h // hb),
        in_specs=[
            pl.BlockSpec((tm, Q_LORA), lambda m, hh: (m, 0)),
            pl.BlockSpec((tm, KV_LORA), lambda m, hh: (m, 1)),
            pl.BlockSpec((2 * QK_ROPE, tm), lambda m, hh: (0, m)),
            pl.BlockSpec((1, tm), lambda m, hh: (0, m)),
            pl.BlockSpec((2 * QK_ROPE, 1), lambda m, hh: (0, 0)),
            pl.BlockSpec((1, Q_LORA), lambda m, hh: (0, 0)),
            pl.BlockSpec((1, KV_LORA), lambda m, hh: (0, 0)),
            pl.BlockSpec((hb * QK_PAD, Q_LORA), lambda m, hh: (hh, 0)),
            pl.BlockSpec((KV_LORA, hb * QK_NOPE), lambda m, hh: (0, hh)),
            pl.BlockSpec((hb * V_DIM, KV_LORA), lambda m, hh: (hh, 0)),
        ],
        out_specs=[
            pl.BlockSpec((1, hb, QK_PAD, tm), lambda m, hh: (m // spb, hh, 0, m % spb)),
            pl.BlockSpec((1, hb, tm, QK_PAD), lambda m, hh: (m // spb, hh, m % spb, 0)),
            pl.BlockSpec((1, hb, VT_ROWS, tm), lambda m, hh: (m // spb, hh, 0, m % spb)),
        ],
        out_shape=[
            jax.ShapeDtypeStruct((batch, h, QK_PAD, seq), BF16),
            jax.ShapeDtypeStruct((batch, h, seq, QK_PAD), BF16),
            jax.ShapeDtypeStruct((batch, h, VT_ROWS, seq), BF16),
        ],
        scratch_shapes=[
            pltpu.VMEM((tm, Q_LORA), BF16),
            pltpu.VMEM((tm, KV_LORA), BF16),
            pltpu.VMEM((2 * QK_ROPE, tm), F32),
            pltpu.VMEM((tm, 2 * QK_ROPE), BF16),
        ],
        compiler_params=_params("parallel", "arbitrary"),
        name="mla_prep",
    )(proj, proj, small_t, pos, invf, q_norm, kv_norm, wq_t, wk, wv_t)


def _attn_kernel(qt_ref, k_ref, vt_ref, o_ref, m_scr, acc_scr, sa_scr, sb_scr, *, tq):
    qi = pl.program_id(2)
    m_scr[...] = jnp.full(m_scr.shape, -jnp.inf, F32)
    acc_scr[...] = jnp.zeros(acc_scr.shape, F32)

    def scores(ki):
        start = pl.multiple_of(ki * tq, tq)
        return _dot(k_ref[0, 0, pl.ds(start, tq), :], qt_ref[0, 0])

    def accumulate(ki, s):
        start = pl.multiple_of(ki * tq, tq)
        m_prev = m_scr[...]
        m_new = jnp.maximum(m_prev, jnp.max(s, axis=0, keepdims=True))
        p = jnp.exp2(s - m_new).astype(BF16)
        acc_scr[...] = jnp.exp2(m_prev - m_new) * acc_scr[...] + _dot(vt_ref[0, 0, :, pl.ds(start, tq)], p)
        m_scr[...] = m_new

    def accumulate_diagonal(s):
        key = lax.broadcasted_iota(jnp.int32, s.shape, 0)
        qry = lax.broadcasted_iota(jnp.int32, s.shape, 1)
        accumulate(qi, jnp.where(key <= qry, s, -jnp.inf))

    sa_scr[...] = scores(0)

    def pair(j, carry):
        sb_scr[...] = scores(2 * j + 1)
        accumulate(2 * j, sa_scr[...])
        sa_scr[...] = scores(2 * j + 2)
        accumulate(2 * j + 1, sb_scr[...])
        return carry

    lax.fori_loop(0, qi // 2, pair, 0)

    @pl.when(qi % 2 == 0)
    def _():
        accumulate_diagonal(sa_scr[...])

    @pl.when(qi % 2 == 1)
    def _():
        sb_scr[...] = scores(qi)
        accumulate(qi - 1, sa_scr[...])
        accumulate_diagonal(sb_scr[...])

    acc = acc_scr[...]
    o_ref[0] = (acc[:V_DIM] / acc[V_DIM:V_DIM + 1]).T.astype(BF16)


def _attention(qt, k, vt, tq):
    b, h, s, _ = k.shape
    return pl.pallas_call(
        functools.partial(_attn_kernel, tq=tq),
        grid=(b, h, s // tq),
        in_specs=[
            pl.BlockSpec((1, 1, QK_PAD, tq), lambda bb, hh, qi: (bb, hh, 0, qi)),
            pl.BlockSpec((1, 1, s, QK_PAD), lambda bb, hh, qi: (bb, hh, 0, 0)),
            pl.BlockSpec((1, 1, VT_ROWS, s), lambda bb, hh, qi: (bb, hh, 0, 0)),
        ],
        out_specs=pl.BlockSpec((1, tq, V_DIM), lambda bb, hh, qi: (bb, qi, hh)),
        out_shape=jax.ShapeDtypeStruct((b, s, h * V_DIM), BF16),
        scratch_shapes=[
            pltpu.VMEM((1, tq), F32),
            pltpu.VMEM((VT_ROWS, tq), F32),
            pltpu.VMEM((tq, tq), F32),
            pltpu.VMEM((tq, tq), F32),
        ],
        compiler_params=_params("parallel", "parallel", "arbitrary"),
        name="attention",
    )(qt, k, vt)


def _split3(x):
    hi = x.astype(BF16).astype(F32)
    r = x - hi
    mid = r.astype(BF16).astype(F32)
    return hi, mid, r - mid


def _ssd_kernel(x_ref, b_ref, c_ref, z_ref, dt_ref, gp_ref, hp_ref, expand_ref, tri_ref, shift_ref, o_ref,
                state_scr, tail_scr):
    L = x_ref.shape[0]

    @pl.when(pl.program_id(2) == 0)
    def _():
        state_scr[...] = jnp.zeros(state_scr.shape, F32)
        tail_scr[...] = jnp.zeros(tail_scr.shape, F32)

    raw = jnp.concatenate([x_ref[...], b_ref[...], c_ref[...]], axis=1)
    gp = gp_ref[0]
    cw = gp[:CONV_WIDTH]
    bias = gp[CONV_WIDTH:CONV_WIDTH + 1]
    dskip = gp[CONV_WIDTH + 1:CONV_WIDTH + 2, :GROUP_WIDTH]
    norm_w = gp[CONV_WIDTH + 2:CONV_WIDTH + 3, :GROUP_WIDTH]
    hp = hp_ref[0]
    shifted = _dot(shift_ref[...], raw)
    rawf = raw.astype(F32)
    conv = bias + cw[CONV_WIDTH - 1:CONV_WIDTH, :] * rawf
    for k in range(1, CONV_WIDTH):
        conv = conv + cw[CONV_WIDTH - 1 - k:CONV_WIDTH - k, :] * shifted[(k - 1) * L:k * L]
    ext = jnp.concatenate([tail_scr[...], rawf[:CONV_TAIL]], axis=0)
    head = bias
    for tap in range(CONV_WIDTH):
        off = CONV_TAIL - (CONV_WIDTH - 1) + tap
        head = head + cw[tap:tap + 1, :] * ext[off:off + CONV_TAIL]
    conv = jnp.concatenate([head, conv[CONV_TAIL:]], axis=0)
    tail_scr[...] = rawf[L - CONV_TAIL:, :]
    act = _silu(conv)
    xc = act[:, :GROUP_WIDTH]
    bc = act[:, GROUP_WIDTH:GROUP_WIDTH + D_STATE]
    cc = act[:, GROUP_WIDTH + D_STATE:]

    dtr = dt_ref[...] + hp[:, 0:1]
    dt = jnp.maximum(dtr, 0.0) + jnp.log1p(jnp.exp(-jnp.abs(dtr)))
    da = dt * (-jnp.exp(hp[:, 1:2])) * LOG2_E
    d_hi, d_mid, d_lo = _split3(da)
    tri = tri_ref[...]
    cum = _dot(d_hi.astype(BF16), tri) + _dot(d_mid.astype(BF16), tri) + _dot(d_lo.astype(BF16), tri)

    t_hi, t_mid, t_lo = _split3(dt)
    c_hi, c_mid, c_lo = _split3(cum)
    stacked = jnp.concatenate(
        [t_hi, t_mid, t_lo, c_hi, c_mid, c_lo, cum, jnp.zeros((128 - 7 * HEADS_PER_GROUP, L), F32)], axis=0)
    cols = stacked.T
    expanded = _dot(cols.astype(BF16), expand_ref[...])
    dt_e = expanded[:, :GROUP_WIDTH]
    cum_e = expanded[:, GROUP_WIDTH:]

    xdt = xc * dt_e
    xdt_b = xdt.astype(BF16)
    bc_b = bc.astype(BF16)
    cc_b = cc.astype(BF16)
    last_e = cum_e[L - 1:L, :]
    state = state_scr[...]

    y = _dot(cc_b, state.astype(BF16)) * jnp.exp2(cum_e)
    y = y + dskip * xc

    w_end = jnp.exp2(last_e - cum_e)
    state_scr[...] = state * jnp.exp2(last_e) + _dot(bc.T.astype(BF16), (xdt * w_end).astype(BF16))

    cb = _dot_nt(cc_b, bc_b)
    ri = lax.broadcasted_iota(jnp.int32, (L, L), 0)
    ci = lax.broadcasted_iota(jnp.int32, (L, L), 1)
    causal = ci <= ri
    lane = lax.broadcasted_iota(jnp.int32, (L, 128), 1)
    pairs = []
    for pr in range(HEADS_PER_GROUP // 2):
        xp = xdt_b[:, pr * 128:(pr + 1) * 128]
        acc = None
        for sub in range(2):
            r = 2 * pr + sub
            col = cols[:, 6 * HEADS_PER_GROUP + r:6 * HEADS_PER_GROUP + r + 1]
            seg = col - cum[r:r + 1, :]
            m = (cb * jnp.exp2(jnp.where(causal, seg, -jnp.inf))).astype(BF16)
            keep = (lane < SSM_HEADDIM) if sub == 0 else (lane >= SSM_HEADDIM)
            part = _dot(m, jnp.where(keep, xp, jnp.zeros_like(xp)))
            acc = part if acc is None else acc + part
        pairs.append(acc)
    y = y + jnp.concatenate(pairs, axis=1)

    y = y * _silu(z_ref[...].astype(F32))
    o_ref[...] = _rms(y, norm_w).astype(BF16)


def _shift_matrices():
    L = CHUNK
    shift = np.zeros(((CONV_WIDTH - 1) * L, L), np.float32)
    for k in range(1, CONV_WIDTH):
        for t in range(k, L):
            shift[(k - 1) * L + t, t - k] = 1.0
    return shift


def _ssd_group_params(conv_w, conv_b, d_skip, ssm_norm, dt_bias, a_log):
    g, gw, n = SSM_GROUPS, GROUP_WIDTH, D_STATE
    d_inner = g * gw
    per_group = lambda v: jnp.concatenate(
        [v[:, :d_inner].reshape(-1, g, gw), v[:, d_inner:d_inner + g * n].reshape(-1, g, n),
         v[:, d_inner + g * n:].reshape(-1, g, n)], axis=2).transpose(1, 0, 2)
    pad = lambda v: jnp.pad(v.reshape(g, 1, gw), ((0, 0), (0, 0), (0, 2 * n)))
    gp = jnp.concatenate([per_group(conv_w.astype(F32)), per_group(conv_b.astype(F32).reshape(1, -1)),
                          pad(jnp.repeat(d_skip.astype(F32), SSM_HEADDIM)), pad(ssm_norm.astype(F32)),
                          jnp.zeros((g, 8 - CONV_WIDTH - 3, gw + 2 * n), F32)], axis=1)
    hp = jnp.stack([dt_bias.astype(F32).reshape(g, HEADS_PER_GROUP), a_log.astype(F32).reshape(g, HEADS_PER_GROUP)],
                   axis=2)
    hp = jnp.pad(hp, ((0, 0), (0, 0), (0, 128 - hp.shape[2])))
    return gp, hp


def _ssd(proj, dt_t, gp, hp, expand, tri, shift, batch, seq, offs):
    t = proj.shape[0]
    L = CHUNK
    nc = seq // L
    g = SSM_GROUPS
    d_inner = g * GROUP_WIDTH
    row = lambda bb, gg, c: bb * nc + c
    z_blk, x_blk, b_blk, c_blk, dt_blk = offs
    return pl.pallas_call(
        _ssd_kernel,
        grid=(batch, g, nc),
        in_specs=[
            pl.BlockSpec((L, GROUP_WIDTH), lambda bb, gg, c: (row(bb, gg, c), x_blk + gg)),
            pl.BlockSpec((L, D_STATE), lambda bb, gg, c: (row(bb, gg, c), b_blk + gg)),
            pl.BlockSpec((L, D_STATE), lambda bb, gg, c: (row(bb, gg, c), c_blk + gg)),
            pl.BlockSpec((L, GROUP_WIDTH), lambda bb, gg, c: (row(bb, gg, c), z_blk + gg)),
            pl.BlockSpec((HEADS_PER_GROUP, L), lambda bb, gg, c: (dt_blk + gg, row(bb, gg, c))),
            pl.BlockSpec((1,) + gp.shape[1:], lambda bb, gg, c: (gg, 0, 0)),
            pl.BlockSpec((1,) + hp.shape[1:], lambda bb, gg, c: (gg, 0, 0)),
            pl.BlockSpec((128, 2 * GROUP_WIDTH), lambda bb, gg, c: (0, 0)),
            pl.BlockSpec((L, L), lambda bb, gg, c: (0, 0)),
            pl.BlockSpec(((CONV_WIDTH - 1) * L, L), lambda bb, gg, c: (0, 0)),
        ],
        out_specs=pl.BlockSpec((L, GROUP_WIDTH), lambda bb, gg, c: (row(bb, gg, c), gg)),
        out_shape=jax.ShapeDtypeStruct((t, d_inner), BF16),
        scratch_shapes=[
            pltpu.VMEM((D_STATE, GROUP_WIDTH), F32),
            pltpu.VMEM((CONV_TAIL, GROUP_WIDTH + 2 * D_STATE), F32),
        ],
        compiler_params=_params("parallel", "parallel", "arbitrary"),
        name="ssd",
    )(proj, proj, proj, proj, dt_t, gp, hp, expand, tri, shift)


def _mix_kernel(attn_ref, ssm_ref, ga_ref, gs_ref, wa_ref, ws_ref, o_ref):
    a = _dot(attn_ref[...], wa_ref[...])
    s = _dot(ssm_ref[...], ws_ref[...])
    mixed = _sigmoid(ga_ref[...].astype(F32)) * a + _sigmoid(gs_ref[...].astype(F32)) * s
    o_ref[...] = mixed.astype(BF16)


def _mix(attn, ssm, proj, wa, ws, ga_blk, gs_blk, tm, tn):
    t = attn.shape[0]
    d = wa.shape[1]
    return pl.pallas_call(
        _mix_kernel,
        grid=(t // tm, d // tn),
        in_specs=[
            pl.BlockSpec((tm, attn.shape[1]), lambda m, n: (m, 0)),
            pl.BlockSpec((tm, ssm.shape[1]), lambda m, n: (m, 0)),
            pl.BlockSpec((tm, tn), lambda m, n: (m, ga_blk + n)),
            pl.BlockSpec((tm, tn), lambda m, n: (m, gs_blk + n)),
            pl.BlockSpec((wa.shape[0], tn), lambda m, n: (0, n)),
            pl.BlockSpec((ws.shape[0], tn), lambda m, n: (0, n)),
        ],
        out_specs=pl.BlockSpec((tm, tn), lambda m, n: (m, n)),
        out_shape=jax.ShapeDtypeStruct((t, d), BF16),
        compiler_params=_params("parallel", "arbitrary"),
        name="mix",
    )(attn, ssm, proj, proj, wa, ws)


def _mix_out_kernel(mixed_ref, x_ref, w_ref, post_ref, pre_ref, h_ref, f_ref):
    o = _dot(mixed_ref[...], w_ref[...])
    h = x_ref[...] + _rms(o, post_ref[...])
    h_ref[...] = h
    f_ref[...] = _rms(h, pre_ref[...]).astype(BF16)


def _mix_out(mixed, x2, w_out, post_w, pre_w, tm):
    t, d = x2.shape
    row = pl.BlockSpec((tm, d), lambda m: (m, 0))
    vec = pl.BlockSpec((1, d), lambda m: (0, 0))
    return pl.pallas_call(
        _mix_out_kernel,
        grid=(t // tm,),
        in_specs=[row, row, pl.BlockSpec((d, d), lambda m: (0, 0)), vec, vec],
        out_specs=[row, row],
        out_shape=[jax.ShapeDtypeStruct((t, d), F32), jax.ShapeDtypeStruct((t, d), BF16)],
        compiler_params=_params("parallel"),
        name="mix_out",
    )(mixed, x2, w_out, post_w, pre_w)


def _ffn_kernel(f_ref, wg_ref, wu_ref, wd_ref, o_ref, acc_scr):
    kf = pl.program_id(1)

    @pl.when(kf == 0)
    def _():
        acc_scr[...] = jnp.zeros(acc_scr.shape, F32)

    f = f_ref[...]
    g = _dot(f, wg_ref[...])
    u = _dot(f, wu_ref[...])
    a = (_silu(g) * u).astype(BF16)
    acc_scr[...] += _dot(a, wd_ref[...])

    @pl.when(kf == pl.num_programs(1) - 1)
    def _():
        o_ref[...] = acc_scr[...].astype(BF16)


def _ffn(f, wg, wu, wd, tm, tf):
    t, d = f.shape
    dff = wg.shape[1]
    return pl.pallas_call(
        _ffn_kernel,
        grid=(t // tm, dff // tf),
        in_specs=[
            pl.BlockSpec((tm, d), lambda m, k: (m, 0)),
            pl.BlockSpec((d, tf), lambda m, k: (0, k)),
            pl.BlockSpec((d, tf), lambda m, k: (0, k)),
            pl.BlockSpec((tf, d), lambda m, k: (k, 0)),
        ],
        out_specs=pl.BlockSpec((tm, d), lambda m, k: (m, 0)),
        out_shape=jax.ShapeDtypeStruct((t, d), BF16),
        scratch_shapes=[pltpu.VMEM((tm, d), F32)],
        compiler_params=_params("parallel", "arbitrary"),
        name="ffn",
    )(f, wg, wu, wd)


def _ple_kernel(h_ref, d_ref, p_ref, fpost_ref, ppre_ref, ppost_ref, wg_ref, wp_ref, o_ref):
    h = h_ref[...] + _rms(d_ref[...].astype(F32), fpost_ref[...])
    hn = _rms(h, ppre_ref[...]).astype(BF16)
    gate = _sigmoid(_dot(hn, wg_ref[...]))
    e = _dot(p_ref[...].astype(BF16), wp_ref[...]) * gate
    o_ref[...] = h + _rms(e, ppost_ref[...])


def _ple(h1, dff, p2, fpost, ppre, ppost, wg, wp, tm):
    t, d = h1.shape
    row = pl.BlockSpec((tm, d), lambda m: (m, 0))
    vec = pl.BlockSpec((1, d), lambda m: (0, 0))
    return pl.pallas_call(
        _ple_kernel,
        grid=(t // tm,),
        in_specs=[row, row, pl.BlockSpec((tm, p2.shape[1]), lambda m: (m, 0)), vec, vec, vec,
                  pl.BlockSpec((d, d), lambda m: (0, 0)),
                  pl.BlockSpec((p2.shape[1], d), lambda m: (0, 0))],
        out_specs=row,
        out_shape=jax.ShapeDtypeStruct((t, d), F32),
        compiler_params=_params("parallel"),
        name="ple",
    )(h1, dff, p2, fpost, ppre, ppost, wg, wp)


def _expand_matrix():
    e = np.zeros((128, 2 * GROUP_WIDTH), np.float32)
    for part in range(3):
        for r in range(HEADS_PER_GROUP):
            lo = r * SSM_HEADDIM
            e[part * HEADS_PER_GROUP + r, lo:lo + SSM_HEADDIM] = 1.0
            e[(3 + part) * HEADS_PER_GROUP + r, GROUP_WIDTH + lo:GROUP_WIDTH + lo + SSM_HEADDIM] = 1.0
    return e


def _layer(h, p_i, pos, invf, mix_norm_pre, mix_norm_post, w_in, q_norm, w_uq, kv_norm, w_ukv,
           conv_w, conv_b, dt_bias, a_log, d_skip, ssm_norm, w_attn_o, w_ssm_o, w_out,
           ffn_norm_pre, ffn_norm_post, w_gate, w_up, w_down,
           ple_norm_pre, ple_norm_post, w_ple_gate, w_ple):
    batch, seq, d = h.shape
    t = batch * seq
    d_inner = SSM_GROUPS * GROUP_WIDTH
    bc_width = SSM_GROUPS * D_STATE
    n_ssm_heads = SSM_GROUPS * HEADS_PER_GROUP
    vec = lambda v: v.reshape(1, -1).astype(F32)

    o_kr = Q_LORA + KV_LORA
    o_z = o_kr + QK_ROPE
    o_xbc = o_z + d_inner
    o_dt = o_xbc + d_inner + 2 * bc_width
    o_ga = o_dt + n_ssm_heads
    w_segs = [w_in[:, :o_kr].astype(BF16), w_in[:, o_z:o_dt].astype(BF16), w_in[:, o_ga:].astype(BF16)]
    half = QK_ROPE // 2
    w_small_t = jnp.concatenate([w_in[:, o_kr:o_z], w_in[:, o_kr + half:o_z], w_in[:, o_kr:o_kr + half],
                                 w_in[:, o_dt:o_ga]], axis=1).T.astype(BF16)
    dt_blk = 2 * QK_ROPE // HEADS_PER_GROUP
    z_blk = (Q_LORA + KV_LORA) // GROUP_WIDTH
    x_blk = z_blk + SSM_GROUPS
    b_blk = (Q_LORA + KV_LORA + 2 * d_inner) // D_STATE
    c_blk = b_blk + SSM_GROUPS
    g_off = Q_LORA + KV_LORA + 2 * d_inner + 2 * bc_width

    x2 = h.reshape(t, d)
    proj, small_t = _inproj(x2, vec(mix_norm_pre), w_segs, w_small_t, tm=1024, tn=1024)

    wq = w_uq.reshape(Q_LORA, N_HEADS_MLA, QK_DIM)
    wq = jnp.concatenate([wq, wq[:, :, QK_NOPE + half:], wq[:, :, QK_NOPE:QK_NOPE + half]], axis=2)
    wq_t = jnp.transpose(wq, (1, 2, 0)).reshape(N_HEADS_MLA * QK_PAD, Q_LORA).astype(BF16)
    wkv = w_ukv.reshape(KV_LORA, N_HEADS_MLA, QK_NOPE + V_DIM)
    wk = wkv[:, :, :QK_NOPE].reshape(KV_LORA, N_HEADS_MLA * QK_NOPE).astype(BF16)
    wv_t = jnp.transpose(wkv[:, :, QK_NOPE:], (1, 2, 0)).reshape(N_HEADS_MLA * V_DIM, KV_LORA).astype(BF16)
    qt, k, vt = _mla_prep(proj, small_t, pos, invf, vec(q_norm), vec(kv_norm), wq_t, wk, wv_t,
                          batch, seq, tm=512, hb=4)
    attn = _attention(qt, k, vt, tq=1024).reshape(t, N_HEADS_MLA * V_DIM)

    expand = jnp.asarray(_expand_matrix(), BF16)
    tri = jnp.asarray(np.triu(np.ones((CHUNK, CHUNK), np.float32)), BF16)
    gp, hp = _ssd_group_params(conv_w, conv_b, d_skip, ssm_norm, dt_bias, a_log)
    ssm = _ssd(proj, small_t, gp, hp, expand, tri, jnp.asarray(_shift_matrices(), BF16), batch, seq,
               (z_blk, x_blk, b_blk, c_blk, dt_blk))

    tn_mix = 512
    mixed = _mix(attn, ssm, proj, w_attn_o.astype(BF16), w_ssm_o.astype(BF16),
                 g_off // tn_mix, (g_off + d) // tn_mix, tm=1024, tn=tn_mix)
    h1, f = _mix_out(mixed, x2, w_out.astype(BF16), vec(mix_norm_post), vec(ffn_norm_pre), tm=512)
    dff = _ffn(f, w_gate.astype(BF16), w_up.astype(BF16), w_down.astype(BF16), tm=1024, tf=512)
    out = _ple(h1, dff, p_i.reshape(t, -1), vec(ffn_norm_post), vec(ple_norm_pre), vec(ple_norm_post),
               w_ple_gate.astype(BF16), w_ple.astype(BF16), tm=512)
    return out.reshape(batch, seq, d)


def kernel(x, p, positions, mix_norm_pre, mix_norm_post, w_in, q_norm, w_uq, kv_norm, w_ukv, conv_w, conv_b, dt_bias, a_log, d_skip, ssm_norm, w_attn_o, w_ssm_o, w_out, ffn_norm_pre, ffn_norm_post, w_gate, w_up, w_down, ple_norm_pre, ple_norm_post, w_ple_gate, w_ple):
    invf = ROPE_THETA ** (-jnp.arange(0, QK_ROPE, 2, dtype=F32) / QK_ROPE)
    invf = jnp.tile(invf, 2 * QK_ROPE // invf.shape[0]).reshape(2 * QK_ROPE, 1)
    pos = positions.reshape(1, -1)
    h = x
    for i in range(w_in.shape[0]):
        h = _layer(h, p[i], pos, invf, mix_norm_pre[i], mix_norm_post[i], w_in[i], q_norm[i], w_uq[i],
                   kv_norm[i], w_ukv[i], conv_w[i], conv_b[i], dt_bias[i], a_log[i], d_skip[i], ssm_norm[i],
                   w_attn_o[i], w_ssm_o[i], w_out[i], ffn_norm_pre[i], ffn_norm_post[i],
                   w_gate[i], w_up[i], w_down[i], ple_norm_pre[i], ple_norm_post[i], w_ple_gate[i], w_ple[i])
    return h
```

```python
import functools

import jax
import jax.numpy as jnp
import numpy as np
from jax import lax
from jax.experimental import pallas as pl
from jax.experimental.pallas import tpu as pltpu

F32 = jnp.float32
BF16 = jnp.bfloat16

EPS = 1e-6
ROPE_THETA = 10000.0
LOG2_E = 1.4426950408889634

N_HEADS_MLA = 16
Q_LORA = 512
KV_LORA = 512
QK_NOPE = 128
QK_ROPE = 64
V_DIM = 128
QK_DIM = QK_NOPE + QK_ROPE
QK_PAD = 256
VT_ROWS = V_DIM + 16

SSM_HEADDIM = 64
SSM_GROUPS = 8
HEADS_PER_GROUP = 8
GROUP_WIDTH = HEADS_PER_GROUP * SSM_HEADDIM
D_STATE = 128
CONV_WIDTH = 4
CHUNK = 256
CONV_TAIL = 8

VMEM_LIMIT = 56 * 1024 * 1024


def _params(*sem):
    return pltpu.CompilerParams(dimension_semantics=sem, vmem_limit_bytes=VMEM_LIMIT)


def _rms(x, w):
    return x * lax.rsqrt(jnp.mean(x * x, axis=-1, keepdims=True) + EPS) * w


def _sigmoid(x):
    return 0.5 + 0.5 * jnp.tanh(0.5 * x)


def _silu(x):
    h = 0.5 * x
    return h + h * jnp.tanh(h)


def _dot(a, b):
    return jnp.dot(a, b, preferred_element_type=F32)


def _dot_nt(a, b):
    return lax.dot_general(a, b, (((1,), (1,)), ((), ())), preferred_element_type=F32)


def _inproj_kernel(x_ref, nw_ref, *refs, bounds):
    nseg = len(bounds) - 1
    w_refs, (wt_ref, o_ref, ot_ref, u_scr) = refs[:nseg], refs[nseg:]
    n = pl.program_id(1)

    @pl.when(n == 0)
    def _():
        u = _rms(x_ref[...], nw_ref[...]).astype(BF16)
        u_scr[...] = u
        ot_ref[...] = _dot_nt(wt_ref[...], u)

    for seg in range(nseg):
        @pl.when((n >= bounds[seg]) & (n < bounds[seg + 1]))
        def _(w_ref=w_refs[seg]):
            o_ref[...] = _dot(u_scr[...], w_ref[...]).astype(BF16)


def _inproj(x2, norm_w, w_segs, w_small_t, tm, tn):
    t, d = x2.shape
    bounds = [0]
    for w in w_segs:
        bounds.append(bounds[-1] + w.shape[1] // tn)
    n_main = bounds[-1] * tn
    n_small = w_small_t.shape[0]

    def seg_spec(lo, hi):
        return pl.BlockSpec((d, tn), lambda m, n: (0, jnp.clip(n - lo, 0, hi - lo - 1)))

    return pl.pallas_call(
        functools.partial(_inproj_kernel, bounds=tuple(bounds)),
        grid=(t // tm, n_main // tn),
        in_specs=[
            pl.BlockSpec((tm, d), lambda m, n: (m, 0)),
            pl.BlockSpec((1, d), lambda m, n: (0, 0)),
            *[seg_spec(bounds[i], bounds[i + 1]) for i in range(len(w_segs))],
            pl.BlockSpec((n_small, d), lambda m, n: (0, 0)),
        ],
        out_specs=[
            pl.BlockSpec((tm, tn), lambda m, n: (m, n)),
            pl.BlockSpec((n_small, tm), lambda m, n: (0, m)),
        ],
        out_shape=[
            jax.ShapeDtypeStruct((t, n_main), BF16),
            jax.ShapeDtypeStruct((n_small, t), F32),
        ],
        scratch_shapes=[pltpu.VMEM((tm, d), BF16)],
        compiler_params=_params("parallel", "arbitrary"),
        name="inproj",
    )(x2, norm_w, *w_segs, w_small_t)


def _mla_prep_kernel(cq_ref, ckv_ref, kr_ref, pos_ref, invf_ref, qn_ref, kvn_ref, wq_ref, wk_ref, wv_ref,
                     qt_ref, k_ref, vt_ref, cqn_scr, ckvn_scr, tab_scr, kpe_scr):
    def rope_t(pair):
        t = pair * tab_scr[...]
        return t[:QK_ROPE] + t[QK_ROPE:]

    @pl.when(pl.program_id(1) == 0)
    def _():
        cqn_scr[...] = _rms(cq_ref[...].astype(F32), qn_ref[...]).astype(BF16)
        ckvn_scr[...] = _rms(ckv_ref[...].astype(F32), kvn_ref[...]).astype(BF16)
        ang = invf_ref[...] * pos_ref[...].astype(F32)
        row = lax.broadcasted_iota(jnp.int32, ang.shape, 0)
        sin = jnp.sin(ang)
        neg = (row >= QK_ROPE) & (row < QK_ROPE + QK_ROPE // 2)
        tab_scr[...] = jnp.where(row < QK_ROPE, jnp.cos(ang), jnp.where(neg, -sin, sin))
        kpe_t = rope_t(kr_ref[...])
        kpe_scr[...] = jnp.concatenate([kpe_t, jnp.zeros_like(kpe_t)], axis=0).T.astype(BF16)

    scale = QK_DIM ** -0.5 * LOG2_E
    ckvn = ckvn_scr[...]
    qo = _dot_nt(wq_ref[...], cqn_scr[...])
    kn = _dot(ckvn, wk_ref[...]).astype(BF16)
    vo = _dot_nt(wv_ref[...], ckvn).astype(BF16)
    pad_row = lax.broadcasted_iota(jnp.int32, (VT_ROWS - V_DIM, vo.shape[1]), 0)
    ones_rows = jnp.where(pad_row == 0, 1.0, 0.0).astype(BF16)
    for hh in range(qt_ref.shape[1]):
        qh = qo[hh * QK_PAD:(hh + 1) * QK_PAD]
        qt_ref[0, hh, :QK_NOPE, :] = (qh[:QK_NOPE] * scale).astype(BF16)
        q_pe = (rope_t(qh[QK_NOPE:]) * scale).astype(BF16)
        qt_ref[0, hh, QK_NOPE:QK_DIM, :] = q_pe
        qt_ref[0, hh, QK_DIM:, :] = jnp.zeros_like(q_pe)
        k_ref[0, hh, :, :QK_NOPE] = kn[:, hh * QK_NOPE:(hh + 1) * QK_NOPE]
        k_ref[0, hh, :, QK_NOPE:] = kpe_scr[...]
        vt_ref[0, hh, :V_DIM, :] = vo[hh * V_DIM:(hh + 1) * V_DIM]
        vt_ref[0, hh, V_DIM:, :] = ones_rows


def _mla_prep(proj, small_t, pos, invf, q_norm, kv_norm, wq_t, wk, wv_t, batch, seq, tm, hb):
    t = proj.shape[0]
    spb = seq // tm
    h = N_HEADS_MLA
    return pl.pallas_call(
        _mla_prep_kernel,
        grid=(t // tm, h // hb),
        in_specs=[
            pl.BlockSpec((tm, Q_LORA), lambda m, hh: (m, 0)),
            pl.BlockSpec((tm, KV_LORA), lambda m, hh: (m, 1)),
            pl.BlockSpec((2 * QK_ROPE, tm), lambda m, hh: (0, m)),
            pl.BlockSpec((1, tm), lambda m, hh: (0, m)),
            pl.BlockSpec((2 * QK_ROPE, 1), lambda m, hh: (0, 0)),
            pl.BlockSpec((1, Q_LORA), lambda m, hh: (0, 0)),
            pl.BlockSpec((1, KV_LORA), lambda m, hh: (0, 0)),
            pl.BlockSpec((hb * QK_PAD, Q_LORA), lambda m, hh: (hh, 0)),
            pl.BlockSpec((KV_LORA, hb * QK_NOPE), lambda m, hh: (0, hh)),
            pl.BlockSpec((hb * V_DIM, KV_LORA), lambda m, hh: (hh, 0)),
        ],
        out_specs=[
            pl.BlockSpec((1, hb, QK_PAD, tm), lambda m, hh: (m // spb, hh, 0, m % spb)),
            pl.BlockSpec((1, hb, tm, QK_PAD), lambda m, hh: (m // spb, hh, m % spb, 0)),
            pl.BlockSpec((1, hb, VT_ROWS, tm), lambda m, hh: (m // spb, hh, 0, m % spb)),
        ],
        out_shape=[
            jax.ShapeDtypeStruct((batch, h, QK_PAD, seq), BF16),
            jax.ShapeDtypeStruct((batch, h, seq, QK_PAD), BF16),
            jax.ShapeDtypeStruct((batch, h, VT_ROWS, seq), BF16),
        ],
        scratch_shapes=[
            pltpu.VMEM((tm, Q_LORA), BF16),
            pltpu.VMEM((tm, KV_LORA), BF16),
            pltpu.VMEM((2 * QK_ROPE, tm), F32),
            pltpu.VMEM((tm, 2 * QK_ROPE), BF16),
        ],
        compiler_params=_params("parallel", "arbitrary"),
        name="mla_prep",
    )(proj, proj, small_t, pos, invf, q_norm, kv_norm, wq_t, wk, wv_t)


def _attn_kernel(qt_ref, k_ref, vt_ref, o_ref, m_scr, acc_scr, sa_scr, sb_scr, *, tq):
    qi = pl.program_id(2)
    m_scr[...] = jnp.full(m_scr.shape, -jnp.inf, F32)
    acc_scr[...] = jnp.zeros(acc_scr.shape, F32)

    def scores(ki):
        start = pl.multiple_of(ki * tq, tq)
        return _dot(k_ref[0, 0, pl.ds(start, tq), :], qt_ref[0, 0])

    def accumulate(ki, s):
        start = pl.multiple_of(ki * tq, tq)
        m_prev = m_scr[...]
        m_new = jnp.maximum(m_prev, jnp.max(s, axis=0, keepdims=True))
        p = jnp.exp2(s - m_new).astype(BF16)
        acc_scr[...] = jnp.exp2(m_prev - m_new) * acc_scr[...] + _dot(vt_ref[0, 0, :, pl.ds(start, tq)], p)
        m_scr[...] = m_new

    def accumulate_diagonal(s):
        key = lax.broadcasted_iota(jnp.int32, s.shape, 0)
        qry = lax.broadcasted_iota(jnp.int32, s.shape, 1)
        accumulate(qi, jnp.where(key <= qry, s, -jnp.inf))

    sa_scr[...] = scores(0)

    def pair(j, carry):
        sb_scr[...] = scores(2 * j + 1)
        accumulate(2 * j, sa_scr[...])
        sa_scr[...] = scores(2 * j + 2)
        accumulate(2 * j + 1, sb_scr[...])
        return carry

    lax.fori_loop(0, qi // 2, pair, 0)

    @pl.when(qi % 2 == 0)
    def _():
        accumulate_diagonal(sa_scr[...])

    @pl.when(qi % 2 == 1)
    def _():
        sb_scr[...] = scores(qi)
        accumulate(qi - 1, sa_scr[...])
        accumulate_diagonal(sb_scr[...])

    acc = acc_scr[...]
    o_ref[0] = (acc[:V_DIM] / acc[V_DIM:V_DIM + 1]).T.astype(BF16)


def _attention(qt, k, vt, tq):
    b, h, s, _ = k.shape
    return pl.pallas_call(
        functools.partial(_attn_kernel, tq=tq),
        grid=(b, h, s // tq),
        in_specs=[
            pl.BlockSpec((1, 1, QK_PAD, tq), lambda bb, hh, qi: (bb, hh, 0, qi)),
            pl.BlockSpec((1, 1, s, QK_PAD), lambda bb, hh, qi: (bb, hh, 0, 0)),
            pl.BlockSpec((1, 1, VT_ROWS, s), lambda bb, hh, qi: (bb, hh, 0, 0)),
        ],
        out_specs=pl.BlockSpec((1, tq, V_DIM), lambda bb, hh, qi: (bb, qi, hh)),
        out_shape=jax.ShapeDtypeStruct((b, s, h * V_DIM), BF16),
        scratch_shapes=[
            pltpu.VMEM((1, tq), F32),
            pltpu.VMEM((VT_ROWS, tq), F32),
            pltpu.VMEM((tq, tq), F32),
            pltpu.VMEM((tq, tq), F32),
        ],
        compiler_params=_params("parallel", "parallel", "arbitrary"),
        name="attention",
    )(qt, k, vt)


def _split3(x):
    hi = x.astype(BF16).astype(F32)
    r = x - hi
    mid = r.astype(BF16).astype(F32)
    return hi, mid, r - mid


_SSD_SLOT = (("xdt", (CHUNK, GROUP_WIDTH), BF16),
             ("xw", (CHUNK, GROUP_WIDTH), BF16),
             ("ecum", (CHUNK, GROUP_WIDTH), F32),
             ("dsx", (CHUNK, GROUP_WIDTH), F32),
             ("dlast", (1, GROUP_WIDTH), F32),
             ("cc", (CHUNK, D_STATE), BF16),
             ("bc", (CHUNK, D_STATE), BF16),
             ("bct", (D_STATE, CHUNK), BF16),
             ("cum", (HEADS_PER_GROUP, CHUNK), F32),
             ("cols", (CHUNK, 128), F32))


def _ssd_kernel(*refs):
    n_fixed = 21
    (x0_ref, b0_ref, c0_ref, dt0_ref, x1_ref, b1_ref, c1_ref, dt1_ref, x2_ref, b2_ref, c2_ref, dt2_ref,
     z_ref, gp_ref, hp_ref, expand_ref, tri_ref, shift_ref, o_ref, state_scr, tail_scr) = refs[:n_fixed]
    names = [n for n, _, _ in _SSD_SLOT]
    slots = [dict(zip(names, refs[n_fixed + i * len(names):n_fixed + (i + 1) * len(names)])) for i in range(2)]
    L = CHUNK
    gp = gp_ref[0]
    cw = gp[:CONV_WIDTH]
    bias = gp[CONV_WIDTH:CONV_WIDTH + 1]
    dskip = gp[CONV_WIDTH + 1:CONV_WIDTH + 2, :GROUP_WIDTH]
    norm_w = gp[CONV_WIDTH + 2:CONV_WIDTH + 3, :GROUP_WIDTH]
    hp = hp_ref[0]

    def stage_a(x_ref, b_ref, c_ref, dt_ref, out):
        raw = jnp.concatenate([x_ref[...], b_ref[...], c_ref[...]], axis=1)
        shifted = _dot(shift_ref[...], raw)
        rawf = raw.astype(F32)
        conv = bias + cw[CONV_WIDTH - 1:CONV_WIDTH, :] * rawf
        for k in range(1, CONV_WIDTH):
            conv = conv + cw[CONV_WIDTH - 1 - k:CONV_WIDTH - k, :] * shifted[(k - 1) * L:k * L]
        ext = jnp.concatenate([tail_scr[...], rawf[:CONV_TAIL]], axis=0)
        head = bias
        for tap in range(CONV_WIDTH):
            off = CONV_TAIL - (CONV_WIDTH - 1) + tap
            head = head + cw[tap:tap + 1, :] * ext[off:off + CONV_TAIL]
        conv = jnp.concatenate([head, conv[CONV_TAIL:]], axis=0)
        tail_scr[...] = rawf[L - CONV_TAIL:, :]
        act = _silu(conv)
        xc = act[:, :GROUP_WIDTH]
        bc = act[:, GROUP_WIDTH:GROUP_WIDTH + D_STATE]
        cc = act[:, GROUP_WIDTH + D_STATE:]

        dtr = dt_ref[...] + hp[:, 0:1]
        dt = jnp.maximum(dtr, 0.0) + jnp.log1p(jnp.exp(-jnp.abs(dtr)))
        da = dt * (-jnp.exp(hp[:, 1:2])) * LOG2_E
        d_hi, d_mid, d_lo = _split3(da)
        tri = tri_ref[...]
        cum = _dot(d_hi.astype(BF16), tri) + _dot(d_mid.astype(BF16), tri) + _dot(d_lo.astype(BF16), tri)

        t_hi, t_mid, t_lo = _split3(dt)
        c_hi, c_mid, c_lo = _split3(cum)
        stacked = jnp.concatenate(
            [t_hi, t_mid, t_lo, c_hi, c_mid, c_lo, cum, jnp.zeros((128 - 7 * HEADS_PER_GROUP, L), F32)], axis=0)
        cols = stacked.T
        expanded = _dot(cols.astype(BF16), expand_ref[...])
        dt_e = expanded[:, :GROUP_WIDTH]
        cum_e = expanded[:, GROUP_WIDTH:]
        last_e = cum_e[L - 1:L, :]
        xdt = xc * dt_e
        out["xdt"][...] = xdt.astype(BF16)
        out["xw"][...] = (xdt * jnp.exp2(last_e - cum_e)).astype(BF16)
        out["ecum"][...] = jnp.exp2(cum_e)
        out["dsx"][...] = dskip * xc
        out["dlast"][...] = jnp.exp2(last_e)
        out["cc"][...] = cc.astype(BF16)
        out["bc"][...] = bc.astype(BF16)
        out["bct"][...] = bc.T.astype(BF16)
        out["cum"][...] = cum
        out["cols"][...] = cols

    def stage_b(inp, rows):
        state = state_scr[...]
        cc_b = inp["cc"][...]
        y = _dot(cc_b, state.astype(BF16)) * inp["ecum"][...] + inp["dsx"][...]
        state_scr[...] = state * inp["dlast"][...] + _dot(inp["bct"][...], inp["xw"][...])

        cb = _dot_nt(cc_b, inp["bc"][...])
        ri = lax.broadcasted_iota(jnp.int32, (L, L), 0)
        ci = lax.broadcasted_iota(jnp.int32, (L, L), 1)
        causal = ci <= ri
        lane = lax.broadcasted_iota(jnp.int32, (L, 128), 1)
        cum = inp["cum"][...]
        cols = inp["cols"][...]
        pairs = []
        for pr in range(HEADS_PER_GROUP // 2):
            xp = inp["xdt"][:, pr * 128:(pr + 1) * 128]
            acc = None
            for sub in range(2):
                r = 2 * pr + sub
                col = cols[:, 6 * HEADS_PER_GROUP + r:6 * HEADS_PER_GROUP + r + 1]
                seg = col - cum[r:r + 1, :]
                m = (cb * jnp.exp2(jnp.where(causal, seg, -jnp.inf))).astype(BF16)
                keep = (lane < SSM_HEADDIM) if sub == 0 else (lane >= SSM_HEADDIM)
                part = _dot(m, jnp.where(keep, xp, jnp.zeros_like(xp)))
                acc = part if acc is None else acc + part
            pairs.append(acc)
        y = y + jnp.concatenate(pairs, axis=1)
        y = y * _silu(z_ref[rows, :].astype(F32))
        o_ref[rows, :] = _rms(y, norm_w).astype(BF16)

    @pl.when(pl.program_id(2) == 0)
    def _():
        state_scr[...] = jnp.zeros(state_scr.shape, F32)
        tail_scr[...] = jnp.zeros(tail_scr.shape, F32)
        stage_a(x0_ref, b0_ref, c0_ref, dt0_ref, slots[0])

    stage_a(x1_ref, b1_ref, c1_ref, dt1_ref, slots[1])
    stage_b(slots[0], slice(0, L))
    stage_a(x2_ref, b2_ref, c2_ref, dt2_ref, slots[0])
    stage_b(slots[1], slice(L, 2 * L))


def _shift_matrices():
    L = CHUNK
    shift = np.zeros(((CONV_WIDTH - 1) * L, L), np.float32)
    for k in range(1, CONV_WIDTH):
        for t in range(k, L):
            shift[(k - 1) * L + t, t - k] = 1.0
    return shift


def _ssd_group_params(conv_w, conv_b, d_skip, ssm_norm, dt_bias, a_log):
    g, gw, n = SSM_GROUPS, GROUP_WIDTH, D_STATE
    d_inner = g * gw
    per_group = lambda v: jnp.concatenate(
        [v[:, :d_inner].reshape(-1, g, gw), v[:, d_inner:d_inner + g * n].reshape(-1, g, n),
         v[:, d_inner + g * n:].reshape(-1, g, n)], axis=2).transpose(1, 0, 2)
    pad = lambda v: jnp.pad(v.reshape(g, 1, gw), ((0, 0), (0, 0), (0, 2 * n)))
    gp = jnp.concatenate([per_group(conv_w.astype(F32)), per_group(conv_b.astype(F32).reshape(1, -1)),
                          pad(jnp.repeat(d_skip.astype(F32), SSM_HEADDIM)), pad(ssm_norm.astype(F32)),
                          jnp.zeros((g, 8 - CONV_WIDTH - 3, gw + 2 * n), F32)], axis=1)
    hp = jnp.stack([dt_bias.astype(F32).reshape(g, HEADS_PER_GROUP), a_log.astype(F32).reshape(g, HEADS_PER_GROUP)],
                   axis=2)
    hp = jnp.pad(hp, ((0, 0), (0, 0), (0, 128 - hp.shape[2])))
    return gp, hp


def _ssd(proj, dt_t, gp, hp, expand, tri, shift, batch, seq, offs):
    t = proj.shape[0]
    L = CHUNK
    nc = seq // L
    g = SSM_GROUPS
    d_inner = g * GROUP_WIDTH
    row = lambda bb, gg, c: bb * nc + c
    z_blk, x_blk, b_blk, c_blk, dt_blk = offs
    def chunk_specs(chunk_of_step):
        r = lambda bb, gg, s: row(bb, gg, chunk_of_step(s))
        return [
            pl.BlockSpec((L, GROUP_WIDTH), lambda bb, gg, s: (r(bb, gg, s), x_blk + gg)),
            pl.BlockSpec((L, D_STATE), lambda bb, gg, s: (r(bb, gg, s), b_blk + gg)),
            pl.BlockSpec((L, D_STATE), lambda bb, gg, s: (r(bb, gg, s), c_blk + gg)),
            pl.BlockSpec((HEADS_PER_GROUP, L), lambda bb, gg, s: (dt_blk + gg, r(bb, gg, s))),
        ]

    pair_row = lambda bb, gg, s: bb * (nc // 2) + s
    chunk_inputs = (proj, proj, proj, dt_t)
    return pl.pallas_call(
        _ssd_kernel,
        grid=(batch, g, nc // 2),
        in_specs=[
            *chunk_specs(lambda s: 0),
            *chunk_specs(lambda s: 2 * s + 1),
            *chunk_specs(lambda s: jnp.minimum(2 * s + 2, nc - 1)),
            pl.BlockSpec((2 * L, GROUP_WIDTH), lambda bb, gg, s: (pair_row(bb, gg, s), z_blk + gg)),
            pl.BlockSpec((1,) + gp.shape[1:], lambda bb, gg, s: (gg, 0, 0)),
            pl.BlockSpec((1,) + hp.shape[1:], lambda bb, gg, s: (gg, 0, 0)),
            pl.BlockSpec((128, 2 * GROUP_WIDTH), lambda bb, gg, s: (0, 0)),
            pl.BlockSpec((L, L), lambda bb, gg, s: (0, 0)),
            pl.BlockSpec(((CONV_WIDTH - 1) * L, L), lambda bb, gg, s: (0, 0)),
        ],
        out_specs=pl.BlockSpec((2 * L, GROUP_WIDTH), lambda bb, gg, s: (pair_row(bb, gg, s), gg)),
        out_shape=jax.ShapeDtypeStruct((t, d_inner), BF16),
        scratch_shapes=[
            pltpu.VMEM((D_STATE, GROUP_WIDTH), F32),
            pltpu.VMEM((CONV_TAIL, GROUP_WIDTH + 2 * D_STATE), F32),
            *[pltpu.VMEM(shape, dtype) for _ in range(2) for _, shape, dtype in _SSD_SLOT],
        ],
        compiler_params=_params("parallel", "parallel", "arbitrary"),
        name="ssd",
    )(*chunk_inputs, *chunk_inputs, *chunk_inputs, proj, gp, hp, expand, tri, shift)


def _mix_kernel(attn_ref, ssm_ref, ga_ref, gs_ref, wa_ref, ws_ref, o_ref):
    a = _dot(attn_ref[...], wa_ref[...])
    s = _dot(ssm_ref[...], ws_ref[...])
    mixed = _sigmoid(ga_ref[...].astype(F32)) * a + _sigmoid(gs_ref[...].astype(F32)) * s
    o_ref[...] = mixed.astype(BF16)


def _mix(attn, ssm, proj, wa, ws, ga_blk, gs_blk, tm, tn):
    t = attn.shape[0]
    d = wa.shape[1]
    return pl.pallas_call(
        _mix_kernel,
        grid=(t // tm, d // tn),
        in_specs=[
            pl.BlockSpec((tm, attn.shape[1]), lambda m, n: (m, 0)),
            pl.BlockSpec((tm, ssm.shape[1]), lambda m, n: (m, 0)),
            pl.BlockSpec((tm, tn), lambda m, n: (m, ga_blk + n)),
            pl.BlockSpec((tm, tn), lambda m, n: (m, gs_blk + n)),
            pl.BlockSpec((wa.shape[0], tn), lambda m, n: (0, n)),
            pl.BlockSpec((ws.shape[0], tn), lambda m, n: (0, n)),
        ],
        out_specs=pl.BlockSpec((tm, tn), lambda m, n: (m, n)),
        out_shape=jax.ShapeDtypeStruct((t, d), BF16),
        compiler_params=_params("parallel", "arbitrary"),
        name="mix",
    )(attn, ssm, proj, proj, wa, ws)


def _mix_out_kernel(mixed_ref, x_ref, w_ref, post_ref, pre_ref, h_ref, f_ref):
    o = _dot(mixed_ref[...], w_ref[...])
    h = x_ref[...] + _rms(o, post_ref[...])
    h_ref[...] = h
    f_ref[...] = _rms(h, pre_ref[...]).astype(BF16)


def _mix_out(mixed, x2, w_out, post_w, pre_w, tm):
    t, d = x2.shape
    row = pl.BlockSpec((tm, d), lambda m: (m, 0))
    vec = pl.BlockSpec((1, d), lambda m: (0, 0))
    return pl.pallas_call(
        _mix_out_kernel,
        grid=(t // tm,),
        in_specs=[row, row, pl.BlockSpec((d, d), lambda m: (0, 0)), vec, vec],
        out_specs=[row, row],
        out_shape=[jax.ShapeDtypeStruct((t, d), F32), jax.ShapeDtypeStruct((t, d), BF16)],
        compiler_params=_params("parallel"),
        name="mix_out",
    )(mixed, x2, w_out, post_w, pre_w)


def _ffn_kernel(f_ref, wg_ref, wu_ref, wd_ref, o_ref, acc_scr):
    kf = pl.program_id(1)

    @pl.when(kf == 0)
    def _():
        acc_scr[...] = jnp.zeros(acc_scr.shape, F32)

    f = f_ref[...]
    g = _dot(f, wg_ref[...])
    u = _dot(f, wu_ref[...])
    a = (_silu(g) * u).astype(BF16)
    acc_scr[...] += _dot(a, wd_ref[...])

    @pl.when(kf == pl.num_programs(1) - 1)
    def _():
        o_ref[...] = acc_scr[...].astype(BF16)


def _ffn(f, wg, wu, wd, tm, tf):
    t, d = f.shape
    dff = wg.shape[1]
    return pl.pallas_call(
        _ffn_kernel,
        grid=(t // tm, dff // tf),
        in_specs=[
            pl.BlockSpec((tm, d), lambda m, k: (m, 0)),
            pl.BlockSpec((d, tf), lambda m, k: (0, k)),
            pl.BlockSpec((d, tf), lambda m, k: (0, k)),
            pl.BlockSpec((tf, d), lambda m, k: (k, 0)),
        ],
        out_specs=pl.BlockSpec((tm, d), lambda m, k: (m, 0)),
        out_shape=jax.ShapeDtypeStruct((t, d), BF16),
        scratch_shapes=[pltpu.VMEM((tm, d), F32)],
        compiler_params=_params("parallel", "arbitrary"),
        name="ffn",
    )(f, wg, wu, wd)


def _ple_kernel(h_ref, d_ref, p_ref, fpost_ref, ppre_ref, ppost_ref, wg_ref, wp_ref, o_ref):
    h = h_ref[...] + _rms(d_ref[...].astype(F32), fpost_ref[...])
    hn = _rms(h, ppre_ref[...]).astype(BF16)
    gate = _sigmoid(_dot(hn, wg_ref[...]))
    e = _dot(p_ref[...].astype(BF16), wp_ref[...]) * gate
    o_ref[...] = h + _rms(e, ppost_ref[...])


def _ple(h1, dff, p2, fpost, ppre, ppost, wg, wp, tm):
    t, d = h1.shape
    row = pl.BlockSpec((tm, d), lambda m: (m, 0))
    vec = pl.BlockSpec((1, d), lambda m: (0, 0))
    return pl.pallas_call(
        _ple_kernel,
        grid=(t // tm,),
        in_specs=[row, row, pl.BlockSpec((tm, p2.shape[1]), lambda m: (m, 0)), vec, vec, vec,
                  pl.BlockSpec((d, d), lambda m: (0, 0)),
                  pl.BlockSpec((p2.shape[1], d), lambda m: (0, 0))],
        out_specs=row,
        out_shape=jax.ShapeDtypeStruct((t, d), F32),
        compiler_params=_params("parallel"),
        name="ple",
    )(h1, dff, p2, fpost, ppre, ppost, wg, wp)


def _expand_matrix():
    e = np.zeros((128, 2 * GROUP_WIDTH), np.float32)
    for part in range(3):
        for r in range(HEADS_PER_GROUP):
            lo = r * SSM_HEADDIM
            e[part * HEADS_PER_GROUP + r, lo:lo + SSM_HEADDIM] = 1.0
            e[(3 + part) * HEADS_PER_GROUP + r, GROUP_WIDTH + lo:GROUP_WIDTH + lo + SSM_HEADDIM] = 1.0
    return e


def _layer(h, p_i, pos, invf, mix_norm_pre, mix_norm_post, w_in, q_norm, w_uq, kv_norm, w_ukv,
           conv_w, conv_b, dt_bias, a_log, d_skip, ssm_norm, w_attn_o, w_ssm_o, w_out,
           ffn_norm_pre, ffn_norm_post, w_gate, w_up, w_down,
           ple_norm_pre, ple_norm_post, w_ple_gate, w_ple):
    batch, seq, d = h.shape
    t = batch * seq
    d_inner = SSM_GROUPS * GROUP_WIDTH
    bc_width = SSM_GROUPS * D_STATE
    n_ssm_heads = SSM_GROUPS * HEADS_PER_GROUP
    vec = lambda v: v.reshape(1, -1).astype(F32)

    o_kr = Q_LORA + KV_LORA
    o_z = o_kr + QK_ROPE
    o_xbc = o_z + d_inner
    o_dt = o_xbc + d_inner + 2 * bc_width
    o_ga = o_dt + n_ssm_heads
    w_segs = [w_in[:, :o_kr].astype(BF16), w_in[:, o_z:o_dt].astype(BF16), w_in[:, o_ga:].astype(BF16)]
    half = QK_ROPE // 2
    w_small_t = jnp.concatenate([w_in[:, o_kr:o_z], w_in[:, o_kr + half:o_z], w_in[:, o_kr:o_kr + half],
                                 w_in[:, o_dt:o_ga]], axis=1).T.astype(BF16)
    dt_blk = 2 * QK_ROPE // HEADS_PER_GROUP
    z_blk = (Q_LORA + KV_LORA) // GROUP_WIDTH
    x_blk = z_blk + SSM_GROUPS
    b_blk = (Q_LORA + KV_LORA + 2 * d_inner) // D_STATE
    c_blk = b_blk + SSM_GROUPS
    g_off = Q_LORA + KV_LORA + 2 * d_inner + 2 * bc_width

    x2 = h.reshape(t, d)
    proj, small_t = _inproj(x2, vec(mix_norm_pre), w_segs, w_small_t, tm=1024, tn=1024)

    wq = w_uq.reshape(Q_LORA, N_HEADS_MLA, QK_DIM)
    wq = jnp.concatenate([wq, wq[:, :, QK_NOPE + half:], wq[:, :, QK_NOPE:QK_NOPE + half]], axis=2)
    wq_t = jnp.transpose(wq, (1, 2, 0)).reshape(N_HEADS_MLA * QK_PAD, Q_LORA).astype(BF16)
    wkv = w_ukv.reshape(KV_LORA, N_HEADS_MLA, QK_NOPE + V_DIM)
    wk = wkv[:, :, :QK_NOPE].reshape(KV_LORA, N_HEADS_MLA * QK_NOPE).astype(BF16)
    wv_t = jnp.transpose(wkv[:, :, QK_NOPE:], (1, 2, 0)).reshape(N_HEADS_MLA * V_DIM, KV_LORA).astype(BF16)
    qt, k, vt = _mla_prep(proj, small_t, pos, invf, vec(q_norm), vec(kv_norm), wq_t, wk, wv_t,
                          batch, seq, tm=512, hb=4)
    attn = _attention(qt, k, vt, tq=1024).reshape(t, N_HEADS_MLA * V_DIM)

    expand = jnp.asarray(_expand_matrix(), BF16)
    tri = jnp.asarray(np.triu(np.ones((CHUNK, CHUNK), np.float32)), BF16)
    gp, hp = _ssd_group_params(conv_w, conv_b, d_skip, ssm_norm, dt_bias, a_log)
    ssm = _ssd(proj, small_t, gp, hp, expand, tri, jnp.asarray(_shift_matrices(), BF16), batch, seq,
               (z_blk, x_blk, b_blk, c_blk, dt_blk))

    tn_mix = 512
    mixed = _mix(attn, ssm, proj, w_attn_o.astype(BF16), w_ssm_o.astype(BF16),
                 g_off // tn_mix, (g_off + d) // tn_mix, tm=1024, tn=tn_mix)
    h1, f = _mix_out(mixed, x2, w_out.astype(BF16), vec(mix_norm_post), vec(ffn_norm_pre), tm=512)
    dff = _ffn(f, w_gate.astype(BF16), w_up.astype(BF16), w_down.astype(BF16), tm=1024, tf=512)
    out = _ple(h1, dff, p_i.reshape(t, -1), vec(ffn_norm_post), vec(ple_norm_pre), vec(ple_norm_post),
               w_ple_gate.astype(BF16), w_ple.astype(BF16), tm=512)
    return out.reshape(batch, seq, d)


def kernel(x, p, positions, mix_norm_pre, mix_norm_post, w_in, q_norm, w_uq, kv_norm, w_ukv, conv_w, conv_b, dt_bias, a_log, d_skip, ssm_norm, w_attn_o, w_ssm_o, w_out, ffn_norm_pre, ffn_norm_post, w_gate, w_up, w_down, ple_norm_pre, ple_norm_post, w_ple_gate, w_ple):
    invf = ROPE_THETA ** (-jnp.arange(0, QK_ROPE, 2, dtype=F32) / QK_ROPE)
    invf = jnp.tile(invf, 2 * QK_ROPE // invf.shape[0]).reshape(2 * QK_ROPE, 1)
    pos = positions.reshape(1, -1)
    h = x
    for i in range(w_in.shape[0]):
        h = _layer(h, p[i], pos, invf, mix_norm_pre[i], mix_norm_post[i], w_in[i], q_norm[i], w_uq[i],
                   kv_norm[i], w_ukv[i], conv_w[i], conv_b[i], dt_bias[i], a_log[i], d_skip[i], ssm_norm[i],
                   w_attn_o[i], w_ssm_o[i], w_out[i], ffn_norm_pre[i], ffn_norm_post[i],
                   w_gate[i], w_up[i], w_down[i], ple_norm_pre[i], ple_norm_post[i], w_ple_gate[i], w_ple[i])
    return h
```

```python
import functools
import math

import jax
import jax.numpy as jnp
import numpy as np
from jax import lax
from jax.experimental import pallas as pl
from jax.experimental.pallas import tpu as pltpu

F32 = jnp.float32
BF16 = jnp.bfloat16

EPS = 1e-6
ROPE_THETA = 10000.0
LOG2_E = 1.4426950408889634

N_HEADS_MLA = 16
Q_LORA = 512
KV_LORA = 512
QK_NOPE = 128
QK_ROPE = 64
V_DIM = 128
QK_DIM = QK_NOPE + QK_ROPE
QK_PAD = 256
VT_ROWS = V_DIM + 16

SSM_HEADDIM = 64
SSM_GROUPS = 8
HEADS_PER_GROUP = 8
GROUP_WIDTH = HEADS_PER_GROUP * SSM_HEADDIM
D_STATE = 128
CONV_WIDTH = 4
CHUNK = 256
CONV_TAIL = 8

VMEM_LIMIT = 56 * 1024 * 1024


def _params(*sem):
    return pltpu.CompilerParams(dimension_semantics=sem, vmem_limit_bytes=VMEM_LIMIT)


def _rms(x, w):
    return x * lax.rsqrt(jnp.mean(x * x, axis=-1, keepdims=True) + EPS) * w


def _sigmoid(x):
    return 0.5 + 0.5 * jnp.tanh(0.5 * x)


def _silu(x):
    h = 0.5 * x
    return h + h * jnp.tanh(h)


def _dot(a, b):
    return jnp.dot(a, b, preferred_element_type=F32)


def _dot_nt(a, b):
    return lax.dot_general(a, b, (((1,), (1,)), ((), ())), preferred_element_type=F32)


def _inproj_kernel(x_ref, nw_ref, w_ref, wt_ref, o_ref, ot_ref, u_scr):
    @pl.when(pl.program_id(1) == 0)
    def _():
        u = _rms(x_ref[...], nw_ref[...]).astype(BF16)
        u_scr[...] = u
        ot_ref[...] = _dot_nt(wt_ref[...], u)

    o_ref[...] = _dot_nt(u_scr[...], w_ref[...]).astype(BF16)


def _inproj(x2, norm_w, w_t, segments, w_small_t, tm, tn):
    t, d = x2.shape
    bounds = [0]
    for _, count in segments:
        bounds.append(bounds[-1] + count // tn)
    n_main = bounds[-1] * tn
    n_small = w_small_t.shape[0]

    def w_row(n):
        row = segments[0][0] + n * tn
        for (start, _), lo in zip(segments[1:], bounds[1:]):
            row = jnp.where(n >= lo, start + (n - lo) * tn, row)
        return pl.multiple_of(row, math.gcd(tn, *[start for start, _ in segments]))

    return pl.pallas_call(
        _inproj_kernel,
        grid=(t // tm, n_main // tn),
        in_specs=[
            pl.BlockSpec((tm, d), lambda m, n: (m, 0)),
            pl.BlockSpec((1, d), lambda m, n: (0, 0)),
            pl.BlockSpec((pl.Element(tn), pl.Element(d)), lambda m, n: (w_row(n), 0)),
            pl.BlockSpec((n_small, d), lambda m, n: (0, 0)),
        ],
        out_specs=[
            pl.BlockSpec((tm, tn), lambda m, n: (m, n)),
            pl.BlockSpec((n_small, tm), lambda m, n: (0, m)),
        ],
        out_shape=[
            jax.ShapeDtypeStruct((t, n_main), BF16),
            jax.ShapeDtypeStruct((n_small, t), F32),
        ],
        scratch_shapes=[pltpu.VMEM((tm, d), BF16)],
        compiler_params=_params("parallel", "arbitrary"),
        name="inproj",
    )(x2, norm_w, w_t, w_small_t)


def _mla_prep_kernel(cq_ref, ckv_ref, kr_ref, pos_ref, invf_ref, qn_ref, kvn_ref, wq_ref, wk_ref, wv_ref,
                     qt_ref, k_ref, vt_ref, cqn_scr, ckvn_scr, tab_scr, kpe_scr):
    def rope_t(pair):
        t = pair * tab_scr[...]
        return t[:QK_ROPE] + t[QK_ROPE:]

    @pl.when(pl.program_id(1) == 0)
    def _():
        cqn_scr[...] = _rms(cq_ref[...].astype(F32), qn_ref[...]).astype(BF16)
        ckvn_scr[...] = _rms(ckv_ref[...].astype(F32), kvn_ref[...]).astype(BF16)
        ang = invf_ref[...] * pos_ref[...].astype(F32)
        cos, sin = jnp.cos(ang), jnp.sin(ang)
        tab_scr[...] = jnp.concatenate([cos, cos, -sin, sin], axis=0)
        kpe_t = rope_t(kr_ref[...])
        kpe_scr[...] = jnp.concatenate([kpe_t, jnp.zeros_like(kpe_t)], axis=0).T.astype(BF16)

    scale = QK_DIM ** -0.5 * LOG2_E
    ckvn = ckvn_scr[...]
    qo = _dot_nt(wq_ref[...], cqn_scr[...])
    kn = _dot(ckvn, wk_ref[...]).astype(BF16)
    vo = _dot_nt(wv_ref[...], ckvn).astype(BF16)
    pad_row = lax.broadcasted_iota(jnp.int32, (VT_ROWS - V_DIM, vo.shape[1]), 0)
    ones_rows = jnp.where(pad_row == 0, 1.0, 0.0).astype(BF16)
    for hh in range(qt_ref.shape[1]):
        qh = qo[hh * QK_PAD:(hh + 1) * QK_PAD]
        qt_ref[0, hh, :QK_NOPE, :] = (qh[:QK_NOPE] * scale).astype(BF16)
        q_pe = (rope_t(qh[QK_NOPE:]) * scale).astype(BF16)
        qt_ref[0, hh, QK_NOPE:QK_DIM, :] = q_pe
        qt_ref[0, hh, QK_DIM:, :] = jnp.zeros_like(q_pe)
        k_ref[0, hh, :, :QK_NOPE] = kn[:, hh * QK_NOPE:(hh + 1) * QK_NOPE]
        k_ref[0, hh, :, QK_NOPE:] = kpe_scr[...]
        vt_ref[0, hh, :V_DIM, :] = vo[hh * V_DIM:(hh + 1) * V_DIM]
        vt_ref[0, hh, V_DIM:, :] = ones_rows


def _mla_prep(proj, small_t, pos, invf, q_norm, kv_norm, wq_t, wk, wv_t, batch, seq, tm, hb):
    t = proj.shape[0]
    spb = seq // tm
    h = N_HEADS_MLA
    return pl.pallas_call(
        _mla_prep_kernel,
        grid=(t // tm, h // hb),
        in_specs=[
            pl.BlockSpec((tm, Q_LORA), lambda m, hh: (m, 0)),
            pl.BlockSpec((tm, KV_LORA), lambda m, hh: (m, 1)),
            pl.BlockSpec((2 * QK_ROPE, tm), lambda m, hh: (0, m)),
            pl.BlockSpec((1, tm), lambda m, hh: (0, m)),
            pl.BlockSpec((QK_ROPE // 2, 1), lambda m, hh: (0, 0)),
            pl.BlockSpec((1, Q_LORA), lambda m, hh: (0, 0)),
            pl.BlockSpec((1, KV_LORA), lambda m, hh: (0, 0)),
            pl.BlockSpec((hb * QK_PAD, Q_LORA), lambda m, hh: (hh, 0)),
            pl.BlockSpec((KV_LORA, hb * QK_NOPE), lambda m, hh: (0, hh)),
            pl.BlockSpec((hb * V_DIM, KV_LORA), lambda m, hh: (hh, 0)),
        ],
        out_specs=[
            pl.BlockSpec((1, hb, QK_PAD, tm), lambda m, hh: (m // spb, hh, 0, m % spb)),
            pl.BlockSpec((1, hb, tm, QK_PAD), lambda m, hh: (m // spb, hh, m % spb, 0)),
            pl.BlockSpec((1, hb, VT_ROWS, tm), lambda m, hh: (m // spb, hh, 0, m % spb)),
        ],
        out_shape=[
            jax.ShapeDtypeStruct((batch, h, QK_PAD, seq), BF16),
            jax.ShapeDtypeStruct((batch, h, seq, QK_PAD), BF16),
            jax.ShapeDtypeStruct((batch, h, VT_ROWS, seq), BF16),
        ],
        scratch_shapes=[
            pltpu.VMEM((tm, Q_LORA), BF16),
            pltpu.VMEM((tm, KV_LORA), BF16),
            pltpu.VMEM((2 * QK_ROPE, tm), F32),
            pltpu.VMEM((tm, 2 * QK_ROPE), BF16),
        ],
        compiler_params=_params("parallel", "arbitrary"),
        name="mla_prep",
    )(proj, proj, small_t, pos, invf, q_norm, kv_norm, wq_t, wk, wv_t)


def _attn_kernel(qt_ref, k_ref, vt_ref, o_ref, m_scr, acc_scr, sa_scr, sb_scr, *, tq):
    qi = pl.program_id(2)
    m_scr[...] = jnp.full(m_scr.shape, -jnp.inf, F32)
    acc_scr[...] = jnp.zeros(acc_scr.shape, F32)

    def scores(ki):
        start = pl.multiple_of(ki * tq, tq)
        return _dot(k_ref[0, 0, pl.ds(start, tq), :], qt_ref[0, 0])

    def accumulate(ki, s):
        start = pl.multiple_of(ki * tq, tq)
        m_prev = m_scr[...]
        m_new = jnp.maximum(m_prev, jnp.max(s, axis=0, keepdims=True))
        p = jnp.exp2(s - m_new).astype(BF16)
        acc_scr[...] = jnp.exp2(m_prev - m_new) * acc_scr[...] + _dot(vt_ref[0, 0, :, pl.ds(start, tq)], p)
        m_scr[...] = m_new

    def accumulate_diagonal(s):
        key = lax.broadcasted_iota(jnp.int32, s.shape, 0)
        qry = lax.broadcasted_iota(jnp.int32, s.shape, 1)
        accumulate(qi, jnp.where(key <= qry, s, -jnp.inf))

    sa_scr[...] = scores(0)

    def pair(j, carry):
        sb_scr[...] = scores(2 * j + 1)
        accumulate(2 * j, sa_scr[...])
        sa_scr[...] = scores(2 * j + 2)
        accumulate(2 * j + 1, sb_scr[...])
        return carry

    lax.fori_loop(0, qi // 2, pair, 0)

    @pl.when(qi % 2 == 0)
    def _():
        accumulate_diagonal(sa_scr[...])

    @pl.when(qi % 2 == 1)
    def _():
        sb_scr[...] = scores(qi)
        accumulate(qi - 1, sa_scr[...])
        accumulate_diagonal(sb_scr[...])

    acc = acc_scr[...]
    o_ref[0] = (acc[:V_DIM] / acc[V_DIM:V_DIM + 1]).T.astype(BF16)


def _attention(qt, k, vt, tq):
    b, h, s, _ = k.shape
    return pl.pallas_call(
        functools.partial(_attn_kernel, tq=tq),
        grid=(b, h, s // tq),
        in_specs=[
            pl.BlockSpec((1, 1, QK_PAD, tq), lambda bb, hh, qi: (bb, hh, 0, qi)),
            pl.BlockSpec((1, 1, s, QK_PAD), lambda bb, hh, qi: (bb, hh, 0, 0)),
            pl.BlockSpec((1, 1, VT_ROWS, s), lambda bb, hh, qi: (bb, hh, 0, 0)),
        ],
        out_specs=pl.BlockSpec((1, tq, V_DIM), lambda bb, hh, qi: (bb, qi, hh)),
        out_shape=jax.ShapeDtypeStruct((b, s, h * V_DIM), BF16),
        scratch_shapes=[
            pltpu.VMEM((1, tq), F32),
            pltpu.VMEM((VT_ROWS, tq), F32),
            pltpu.VMEM((tq, tq), F32),
            pltpu.VMEM((tq, tq), F32),
        ],
        compiler_params=_params("parallel", "parallel", "arbitrary"),
        name="attention",
    )(qt, k, vt)


def _split3(x):
    hi = x.astype(BF16).astype(F32)
    r = x - hi
    mid = r.astype(BF16).astype(F32)
    return hi, mid, r - mid


_SSD_SLOT = (("xdt", (CHUNK, GROUP_WIDTH), BF16),
             ("xw", (CHUNK, GROUP_WIDTH), BF16),
             ("ecum", (CHUNK, GROUP_WIDTH), F32),
             ("dsx", (CHUNK, GROUP_WIDTH), F32),
             ("dlast", (1, GROUP_WIDTH), F32),
             ("cc", (CHUNK, D_STATE), BF16),
             ("bc", (CHUNK, D_STATE), BF16),
             ("bct", (D_STATE, CHUNK), BF16),
             ("cum", (HEADS_PER_GROUP, CHUNK), F32),
             ("cols", (CHUNK, 128), F32))


def _ssd_kernel(*refs):
    n_fixed = 21
    (x0_ref, b0_ref, c0_ref, dt0_ref, x1_ref, b1_ref, c1_ref, dt1_ref, x2_ref, b2_ref, c2_ref, dt2_ref,
     z_ref, gp_ref, hp_ref, expand_ref, tri_ref, shift_ref, o_ref, state_scr, tail_scr) = refs[:n_fixed]
    names = [n for n, _, _ in _SSD_SLOT]
    slots = [dict(zip(names, refs[n_fixed + i * len(names):n_fixed + (i + 1) * len(names)])) for i in range(2)]
    L = CHUNK
    gp = gp_ref[0]
    cw = gp[:CONV_WIDTH]
    bias = gp[CONV_WIDTH:CONV_WIDTH + 1]
    dskip = gp[CONV_WIDTH + 1:CONV_WIDTH + 2, :GROUP_WIDTH]
    norm_w = gp[CONV_WIDTH + 2:CONV_WIDTH + 3, :GROUP_WIDTH]
    hp = hp_ref[0]

    def stage_a(x_ref, b_ref, c_ref, dt_ref, out):
        raw = jnp.concatenate([x_ref[...], b_ref[...], c_ref[...]], axis=1)
        shifted = _dot(shift_ref[...], raw)
        rawf = raw.astype(F32)
        conv = bias + cw[CONV_WIDTH - 1:CONV_WIDTH, :] * rawf
        for k in range(1, CONV_WIDTH):
            conv = conv + cw[CONV_WIDTH - 1 - k:CONV_WIDTH - k, :] * shifted[(k - 1) * L:k * L]
        ext = jnp.concatenate([tail_scr[...], rawf[:CONV_TAIL]], axis=0)
        head = bias
        for tap in range(CONV_WIDTH):
            off = CONV_TAIL - (CONV_WIDTH - 1) + tap
            head = head + cw[tap:tap + 1, :] * ext[off:off + CONV_TAIL]
        conv = jnp.concatenate([head, conv[CONV_TAIL:]], axis=0)
        tail_scr[...] = rawf[L - CONV_TAIL:, :]
        act = _silu(conv)
        xc = act[:, :GROUP_WIDTH]
        bc = act[:, GROUP_WIDTH:GROUP_WIDTH + D_STATE]
        cc = act[:, GROUP_WIDTH + D_STATE:]

        dtr = dt_ref[...] + hp[:, 0:1]
        dt = jnp.maximum(dtr, 0.0) + jnp.log1p(jnp.exp(-jnp.abs(dtr)))
        da = dt * (-jnp.exp(hp[:, 1:2])) * LOG2_E
        d_hi, d_mid, d_lo = _split3(da)
        tri = tri_ref[...]
        cum = _dot(d_hi.astype(BF16), tri) + _dot(d_mid.astype(BF16), tri) + _dot(d_lo.astype(BF16), tri)

        t_hi, t_mid, t_lo = _split3(dt)
        c_hi, c_mid, c_lo = _split3(cum)
        stacked = jnp.concatenate(
            [t_hi, t_mid, t_lo, c_hi, c_mid, c_lo, cum, jnp.zeros((128 - 7 * HEADS_PER_GROUP, L), F32)], axis=0)
        cols = stacked.T
        expanded = _dot(cols.astype(BF16), expand_ref[...])
        dt_e = expanded[:, :GROUP_WIDTH]
        cum_e = expanded[:, GROUP_WIDTH:]
        last_e = cum_e[L - 1:L, :]
        xdt = xc * dt_e
        out["xdt"][...] = xdt.astype(BF16)
        out["xw"][...] = (xdt * jnp.exp2(last_e - cum_e)).astype(BF16)
        out["ecum"][...] = jnp.exp2(cum_e)
        out["dsx"][...] = dskip * xc
        out["dlast"][...] = jnp.exp2(last_e)
        out["cc"][...] = cc.astype(BF16)
        out["bc"][...] = bc.astype(BF16)
        out["bct"][...] = bc.T.astype(BF16)
        out["cum"][...] = cum
        out["cols"][...] = cols

    def stage_b(inp, rows):
        state = state_scr[...]
        cc_b = inp["cc"][...]
        y = _dot(cc_b, state.astype(BF16)) * inp["ecum"][...] + inp["dsx"][...]
        state_scr[...] = state * inp["dlast"][...] + _dot(inp["bct"][...], inp["xw"][...])

        cb = _dot_nt(cc_b, inp["bc"][...])
        ri = lax.broadcasted_iota(jnp.int32, (L, L), 0)
        ci = lax.broadcasted_iota(jnp.int32, (L, L), 1)
        causal = ci <= ri
        lane = lax.broadcasted_iota(jnp.int32, (L, 128), 1)
        cum = inp["cum"][...]
        cols = inp["cols"][...]
        pairs = []
        for pr in range(HEADS_PER_GROUP // 2):
            xp = inp["xdt"][:, pr * 128:(pr + 1) * 128]
            acc = None
            for sub in range(2):
                r = 2 * pr + sub
                col = cols[:, 6 * HEADS_PER_GROUP + r:6 * HEADS_PER_GROUP + r + 1]
                seg = col - cum[r:r + 1, :]
                m = (cb * jnp.exp2(jnp.where(causal, seg, -jnp.inf))).astype(BF16)
                keep = (lane < SSM_HEADDIM) if sub == 0 else (lane >= SSM_HEADDIM)
                part = _dot(m, jnp.where(keep, xp, jnp.zeros_like(xp)))
                acc = part if acc is None else acc + part
            pairs.append(acc)
        y = y + jnp.concatenate(pairs, axis=1)
        y = y * _silu(z_ref[rows, :].astype(F32))
        o_ref[rows, :] = _rms(y, norm_w).astype(BF16)

    @pl.when(pl.program_id(2) == 0)
    def _():
        state_scr[...] = jnp.zeros(state_scr.shape, F32)
        tail_scr[...] = jnp.zeros(tail_scr.shape, F32)
        stage_a(x0_ref, b0_ref, c0_ref, dt0_ref, slots[0])

    stage_a(x1_ref, b1_ref, c1_ref, dt1_ref, slots[1])
    stage_b(slots[0], slice(0, L))
    stage_a(x2_ref, b2_ref, c2_ref, dt2_ref, slots[0])
    stage_b(slots[1], slice(L, 2 * L))


def _shift_matrices():
    L = CHUNK
    shift = np.zeros(((CONV_WIDTH - 1) * L, L), np.float32)
    for k in range(1, CONV_WIDTH):
        for t in range(k, L):
            shift[(k - 1) * L + t, t - k] = 1.0
    return shift


def _ssd_group_params(conv_w, conv_b, d_skip, ssm_norm, dt_bias, a_log):
    g, gw, n = SSM_GROUPS, GROUP_WIDTH, D_STATE
    d_inner = g * gw
    per_group = lambda v: jnp.concatenate(
        [v[:, :d_inner].reshape(-1, g, gw), v[:, d_inner:d_inner + g * n].reshape(-1, g, n),
         v[:, d_inner + g * n:].reshape(-1, g, n)], axis=2).transpose(1, 0, 2)
    pad = lambda v: jnp.pad(v.reshape(g, 1, gw), ((0, 0), (0, 0), (0, 2 * n)))
    gp = jnp.concatenate([per_group(conv_w.astype(F32)), per_group(conv_b.astype(F32).reshape(1, -1)),
                          pad(jnp.repeat(d_skip.astype(F32), SSM_HEADDIM)), pad(ssm_norm.astype(F32)),
                          jnp.zeros((g, 8 - CONV_WIDTH - 3, gw + 2 * n), F32)], axis=1)
    hp = jnp.stack([dt_bias.astype(F32).reshape(g, HEADS_PER_GROUP), a_log.astype(F32).reshape(g, HEADS_PER_GROUP)],
                   axis=2)
    hp = jnp.pad(hp, ((0, 0), (0, 0), (0, 128 - hp.shape[2])))
    return gp, hp


def _ssd(proj, dt_t, gp, hp, expand, tri, shift, batch, seq, offs):
    t = proj.shape[0]
    L = CHUNK
    nc = seq // L
    g = SSM_GROUPS
    d_inner = g * GROUP_WIDTH
    row = lambda bb, gg, c: bb * nc + c
    z_blk, x_blk, b_blk, c_blk, dt_blk = offs
    def chunk_specs(chunk_of_step):
        r = lambda bb, gg, s: row(bb, gg, chunk_of_step(s))
        return [
            pl.BlockSpec((L, GROUP_WIDTH), lambda bb, gg, s: (r(bb, gg, s), x_blk + gg)),
            pl.BlockSpec((L, D_STATE), lambda bb, gg, s: (r(bb, gg, s), b_blk + gg)),
            pl.BlockSpec((L, D_STATE), lambda bb, gg, s: (r(bb, gg, s), c_blk + gg)),
            pl.BlockSpec((HEADS_PER_GROUP, L), lambda bb, gg, s: (dt_blk + gg, r(bb, gg, s))),
        ]

    pair_row = lambda bb, gg, s: bb * (nc // 2) + s
    chunk_inputs = (proj, proj, proj, dt_t)
    return pl.pallas_call(
        _ssd_kernel,
        grid=(batch, g, nc // 2),
        in_specs=[
            *chunk_specs(lambda s: 0),
            *chunk_specs(lambda s: 2 * s + 1),
            *chunk_specs(lambda s: jnp.minimum(2 * s + 2, nc - 1)),
            pl.BlockSpec((2 * L, GROUP_WIDTH), lambda bb, gg, s: (pair_row(bb, gg, s), z_blk + gg)),
            pl.BlockSpec((1,) + gp.shape[1:], lambda bb, gg, s: (gg, 0, 0)),
            pl.BlockSpec((1,) + hp.shape[1:], lambda bb, gg, s: (gg, 0, 0)),
            pl.BlockSpec((128, 2 * GROUP_WIDTH), lambda bb, gg, s: (0, 0)),
            pl.BlockSpec((L, L), lambda bb, gg, s: (0, 0)),
            pl.BlockSpec(((CONV_WIDTH - 1) * L, L), lambda bb, gg, s: (0, 0)),
        ],
        out_specs=pl.BlockSpec((2 * L, GROUP_WIDTH), lambda bb, gg, s: (pair_row(bb, gg, s), gg)),
        out_shape=jax.ShapeDtypeStruct((t, d_inner), BF16),
        scratch_shapes=[
            pltpu.VMEM((D_STATE, GROUP_WIDTH), F32),
            pltpu.VMEM((CONV_TAIL, GROUP_WIDTH + 2 * D_STATE), F32),
            *[pltpu.VMEM(shape, dtype) for _ in range(2) for _, shape, dtype in _SSD_SLOT],
        ],
        compiler_params=_params("parallel", "parallel", "arbitrary"),
        name="ssd",
    )(*chunk_inputs, *chunk_inputs, *chunk_inputs, proj, gp, hp, expand, tri, shift)


def _mix_kernel(attn_ref, ssm_ref, ga_ref, gs_ref, wa_ref, ws_ref, o_ref):
    a = _dot(attn_ref[...], wa_ref[...])
    s = _dot(ssm_ref[...], ws_ref[...])
    mixed = _sigmoid(ga_ref[...].astype(F32)) * a + _sigmoid(gs_ref[...].astype(F32)) * s
    o_ref[...] = mixed.astype(BF16)


def _mix(attn, ssm, proj, wa, ws, ga_blk, gs_blk, tm, tn):
    t = attn.shape[0]
    d = wa.shape[1]
    return pl.pallas_call(
        _mix_kernel,
        grid=(t // tm, d // tn),
        in_specs=[
            pl.BlockSpec((tm, attn.shape[1]), lambda m, n: (m, 0)),
            pl.BlockSpec((tm, ssm.shape[1]), lambda m, n: (m, 0)),
            pl.BlockSpec((tm, tn), lambda m, n: (m, ga_blk + n)),
            pl.BlockSpec((tm, tn), lambda m, n: (m, gs_blk + n)),
            pl.BlockSpec((wa.shape[0], tn), lambda m, n: (0, n)),
            pl.BlockSpec((ws.shape[0], tn), lambda m, n: (0, n)),
        ],
        out_specs=pl.BlockSpec((tm, tn), lambda m, n: (m, n)),
        out_shape=jax.ShapeDtypeStruct((t, d), BF16),
        compiler_params=_params("parallel", "arbitrary"),
        name="mix",
    )(attn, ssm, proj, proj, wa, ws)


def _mix_out_kernel(mixed_ref, x_ref, w_ref, post_ref, pre_ref, h_ref, f_ref):
    o = _dot(mixed_ref[...], w_ref[...])
    h = x_ref[...] + _rms(o, post_ref[...])
    h_ref[...] = h
    f_ref[...] = _rms(h, pre_ref[...]).astype(BF16)


def _mix_out(mixed, x2, w_out, post_w, pre_w, tm):
    t, d = x2.shape
    row = pl.BlockSpec((tm, d), lambda m: (m, 0))
    vec = pl.BlockSpec((1, d), lambda m: (0, 0))
    return pl.pallas_call(
        _mix_out_kernel,
        grid=(t // tm,),
        in_specs=[row, row, pl.BlockSpec((d, d), lambda m: (0, 0)), vec, vec],
        out_specs=[row, row],
        out_shape=[jax.ShapeDtypeStruct((t, d), F32), jax.ShapeDtypeStruct((t, d), BF16)],
        compiler_params=_params("parallel"),
        name="mix_out",
    )(mixed, x2, w_out, post_w, pre_w)


def _ffn_kernel(f_ref, wg_ref, wu_ref, wd_ref, o_ref, acc_scr):
    kf = pl.program_id(1)

    @pl.when(kf == 0)
    def _():
        acc_scr[...] = jnp.zeros(acc_scr.shape, F32)

    f = f_ref[...]
    g = _dot(f, wg_ref[...])
    u = _dot(f, wu_ref[...])
    a = (_silu(g) * u).astype(BF16)
    acc_scr[...] += _dot(a, wd_ref[...])

    @pl.when(kf == pl.num_programs(1) - 1)
    def _():
        o_ref[...] = acc_scr[...].astype(BF16)


def _ffn(f, wg, wu, wd, tm, tf):
    t, d = f.shape
    dff = wg.shape[1]
    return pl.pallas_call(
        _ffn_kernel,
        grid=(t // tm, dff // tf),
        in_specs=[
            pl.BlockSpec((tm, d), lambda m, k: (m, 0)),
            pl.BlockSpec((d, tf), lambda m, k: (0, k)),
            pl.BlockSpec((d, tf), lambda m, k: (0, k)),
            pl.BlockSpec((tf, d), lambda m, k: (k, 0)),
        ],
        out_specs=pl.BlockSpec((tm, d), lambda m, k: (m, 0)),
        out_shape=jax.ShapeDtypeStruct((t, d), BF16),
        scratch_shapes=[pltpu.VMEM((tm, d), F32)],
        compiler_params=_params("parallel", "arbitrary"),
        name="ffn",
    )(f, wg, wu, wd)


def _ple_kernel(h_ref, d_ref, p_ref, fpost_ref, ppre_ref, ppost_ref, wg_ref, wp_ref, o_ref):
    h = h_ref[...] + _rms(d_ref[...].astype(F32), fpost_ref[...])
    hn = _rms(h, ppre_ref[...]).astype(BF16)
    gate = _sigmoid(_dot(hn, wg_ref[...]))
    e = _dot(p_ref[...].astype(BF16), wp_ref[...]) * gate
    o_ref[...] = h + _rms(e, ppost_ref[...])


def _ple(h1, dff, p2, fpost, ppre, ppost, wg, wp, tm):
    t, d = h1.shape
    row = pl.BlockSpec((tm, d), lambda m: (m, 0))
    vec = pl.BlockSpec((1, d), lambda m: (0, 0))
    return pl.pallas_call(
        _ple_kernel,
        grid=(t // tm,),
        in_specs=[row, row, pl.BlockSpec((tm, p2.shape[1]), lambda m: (m, 0)), vec, vec, vec,
                  pl.BlockSpec((d, d), lambda m: (0, 0)),
                  pl.BlockSpec((p2.shape[1], d), lambda m: (0, 0))],
        out_specs=row,
        out_shape=jax.ShapeDtypeStruct((t, d), F32),
        compiler_params=_params("parallel"),
        name="ple",
    )(h1, dff, p2, fpost, ppre, ppost, wg, wp)


def _expand_matrix():
    e = np.zeros((128, 2 * GROUP_WIDTH), np.float32)
    for part in range(3):
        for r in range(HEADS_PER_GROUP):
            lo = r * SSM_HEADDIM
            e[part * HEADS_PER_GROUP + r, lo:lo + SSM_HEADDIM] = 1.0
            e[(3 + part) * HEADS_PER_GROUP + r, GROUP_WIDTH + lo:GROUP_WIDTH + lo + SSM_HEADDIM] = 1.0
    return e


def _layer(h, p_i, pos, invf, mix_norm_pre, mix_norm_post, w_in, q_norm, w_uq, kv_norm, w_ukv,
           conv_w, conv_b, dt_bias, a_log, d_skip, ssm_norm, w_attn_o, w_ssm_o, w_out,
           ffn_norm_pre, ffn_norm_post, w_gate, w_up, w_down,
           ple_norm_pre, ple_norm_post, w_ple_gate, w_ple):
    batch, seq, d = h.shape
    t = batch * seq
    d_inner = SSM_GROUPS * GROUP_WIDTH
    bc_width = SSM_GROUPS * D_STATE
    n_ssm_heads = SSM_GROUPS * HEADS_PER_GROUP
    vec = lambda v: v.reshape(1, -1).astype(F32)

    o_kr = Q_LORA + KV_LORA
    o_z = o_kr + QK_ROPE
    o_xbc = o_z + d_inner
    o_dt = o_xbc + d_inner + 2 * bc_width
    o_ga = o_dt + n_ssm_heads
    w_t = w_in.T.astype(BF16)
    segments = ((0, o_kr), (o_z, o_dt - o_z), (o_ga, w_in.shape[1] - o_ga))
    half = QK_ROPE // 2
    w_small_t = jnp.concatenate([w_t[o_kr:o_z], w_t[o_kr + half:o_z], w_t[o_kr:o_kr + half], w_t[o_dt:o_ga]], axis=0)
    dt_blk = 2 * QK_ROPE // HEADS_PER_GROUP
    z_blk = (Q_LORA + KV_LORA) // GROUP_WIDTH
    x_blk = z_blk + SSM_GROUPS
    b_blk = (Q_LORA + KV_LORA + 2 * d_inner) // D_STATE
    c_blk = b_blk + SSM_GROUPS
    g_off = Q_LORA + KV_LORA + 2 * d_inner + 2 * bc_width

    x2 = h.reshape(t, d)
    proj, small_t = _inproj(x2, vec(mix_norm_pre), w_t, segments, w_small_t, tm=1024, tn=1024)

    wq = w_uq.reshape(Q_LORA, N_HEADS_MLA, QK_DIM)
    wq = jnp.concatenate([wq, wq[:, :, QK_NOPE + half:], wq[:, :, QK_NOPE:QK_NOPE + half]], axis=2)
    wq_t = jnp.transpose(wq, (1, 2, 0)).reshape(N_HEADS_MLA * QK_PAD, Q_LORA).astype(BF16)
    wkv = w_ukv.reshape(KV_LORA, N_HEADS_MLA, QK_NOPE + V_DIM)
    wk = wkv[:, :, :QK_NOPE].reshape(KV_LORA, N_HEADS_MLA * QK_NOPE).astype(BF16)
    wv_t = jnp.transpose(wkv[:, :, QK_NOPE:], (1, 2, 0)).reshape(N_HEADS_MLA * V_DIM, KV_LORA).astype(BF16)
    qt, k, vt = _mla_prep(proj, small_t, pos, invf, vec(q_norm), vec(kv_norm), wq_t, wk, wv_t,
                          batch, seq, tm=512, hb=4)
    attn = _attention(qt, k, vt, tq=1024).reshape(t, N_HEADS_MLA * V_DIM)

    expand = jnp.asarray(_expand_matrix(), BF16)
    tri = jnp.asarray(np.triu(np.ones((CHUNK, CHUNK), np.float32)), BF16)
    gp, hp = _ssd_group_params(conv_w, conv_b, d_skip, ssm_norm, dt_bias, a_log)
    ssm = _ssd(proj, small_t, gp, hp, expand, tri, jnp.asarray(_shift_matrices(), BF16), batch, seq,
               (z_blk, x_blk, b_blk, c_blk, dt_blk))

    tn_mix = 512
    mixed = _mix(attn, ssm, proj, w_attn_o.astype(BF16), w_ssm_o.astype(BF16),
                 g_off // tn_mix, (g_off + d) // tn_mix, tm=1024, tn=tn_mix)
    h1, f = _mix_out(mixed, x2, w_out.astype(BF16), vec(mix_norm_post), vec(ffn_norm_pre), tm=512)
    dff = _ffn(f, w_gate.astype(BF16), w_up.astype(BF16), w_down.astype(BF16), tm=1024, tf=512)
    out = _ple(h1, dff, p_i.reshape(t, -1), vec(ffn_norm_post), vec(ple_norm_pre), vec(ple_norm_post),
               w_ple_gate.astype(BF16), w_ple.astype(BF16), tm=512)
    return out.reshape(batch, seq, d)


def kernel(x, p, positions, mix_norm_pre, mix_norm_post, w_in, q_norm, w_uq, kv_norm, w_ukv, conv_w, conv_b, dt_bias, a_log, d_skip, ssm_norm, w_attn_o, w_ssm_o, w_out, ffn_norm_pre, ffn_norm_post, w_gate, w_up, w_down, ple_norm_pre, ple_norm_post, w_ple_gate, w_ple):
    invf = ROPE_THETA ** (-jnp.arange(0, QK_ROPE, 2, dtype=F32) / QK_ROPE)
    invf = invf.reshape(QK_ROPE // 2, 1)
    pos = positions.reshape(1, -1)
    h = x
    for i in range(w_in.shape[0]):
        h = _layer(h, p[i], pos, invf, mix_norm_pre[i], mix_norm_post[i], w_in[i], q_norm[i], w_uq[i],
                   kv_norm[i], w_ukv[i], conv_w[i], conv_b[i], dt_bias[i], a_log[i], d_skip[i], ssm_norm[i],
                   w_attn_o[i], w_ssm_o[i], w_out[i], ffn_norm_pre[i], ffn_norm_post[i],
                   w_gate[i], w_up[i], w_down[i], ple_norm_pre[i], ple_norm_post[i], w_ple_gate[i], w_ple[i])
    return h
```

```python
import functools
import math

import jax
import jax.numpy as jnp
import numpy as np
from jax import lax
from jax.experimental import pallas as pl
from jax.experimental.pallas import tpu as pltpu

F32 = jnp.float32
BF16 = jnp.bfloat16

EPS = 1e-6
ROPE_THETA = 10000.0
LOG2_E = 1.4426950408889634

N_HEADS_MLA = 16
Q_LORA = 512
KV_LORA = 512
QK_NOPE = 128
QK_ROPE = 64
V_DIM = 128
QK_DIM = QK_NOPE + QK_ROPE
QK_PAD = 256
VT_ROWS = V_DIM + 16

SSM_HEADDIM = 64
SSM_GROUPS = 8
HEADS_PER_GROUP = 8
GROUP_WIDTH = HEADS_PER_GROUP * SSM_HEADDIM
D_STATE = 128
CONV_WIDTH = 4
CHUNK = 256
CONV_TAIL = 8

VMEM_LIMIT = 56 * 1024 * 1024


def _params(*sem):
    return pltpu.CompilerParams(dimension_semantics=sem, vmem_limit_bytes=VMEM_LIMIT)


def _rms(x, w):
    return x * lax.rsqrt(jnp.mean(x * x, axis=-1, keepdims=True) + EPS) * w


def _sigmoid(x):
    return 0.5 + 0.5 * jnp.tanh(0.5 * x)


def _silu(x):
    h = 0.5 * x
    return h + h * jnp.tanh(h)


def _dot(a, b):
    return jnp.dot(a, b, preferred_element_type=F32)


ROW_GROUPS = 4


def _row_groups(rows, n):
    step = rows // n
    return [slice(i * step, (i + 1) * step) for i in range(n)]


def _dot_nt(a, b):
    return lax.dot_general(a, b, (((1,), (1,)), ((), ())), preferred_element_type=F32)


def _inproj_kernel(x_ref, nw_ref, w_ref, wt_ref, o_ref, ot_ref, u_scr):
    @pl.when(pl.program_id(1) == 0)
    def _():
        u = _rms(x_ref[...], nw_ref[...]).astype(BF16)
        u_scr[...] = u
        ot_ref[...] = _dot_nt(wt_ref[...], u)

    o_ref[...] = _dot_nt(u_scr[...], w_ref[...]).astype(BF16)


def _inproj(x2, norm_w, w_t, segments, w_small_t, tm, tn):
    t, d = x2.shape
    bounds = [0]
    for _, count in segments:
        bounds.append(bounds[-1] + count // tn)
    n_main = bounds[-1] * tn
    n_small = w_small_t.shape[0]

    def w_row(n):
        row = segments[0][0] + n * tn
        for (start, _), lo in zip(segments[1:], bounds[1:]):
            row = jnp.where(n >= lo, start + (n - lo) * tn, row)
        return pl.multiple_of(row, math.gcd(tn, *[start for start, _ in segments]))

    return pl.pallas_call(
        _inproj_kernel,
        grid=(t // tm, n_main // tn),
        in_specs=[
            pl.BlockSpec((tm, d), lambda m, n: (m, 0)),
            pl.BlockSpec((1, d), lambda m, n: (0, 0)),
            pl.BlockSpec((pl.Element(tn), pl.Element(d)), lambda m, n: (w_row(n), 0)),
            pl.BlockSpec((n_small, d), lambda m, n: (0, 0)),
        ],
        out_specs=[
            pl.BlockSpec((tm, tn), lambda m, n: (m, n)),
            pl.BlockSpec((n_small, tm), lambda m, n: (0, m)),
        ],
        out_shape=[
            jax.ShapeDtypeStruct((t, n_main), BF16),
            jax.ShapeDtypeStruct((n_small, t), F32),
        ],
        scratch_shapes=[pltpu.VMEM((tm, d), BF16)],
        compiler_params=_params("parallel", "arbitrary"),
        name="inproj",
    )(x2, norm_w, w_t, w_small_t)


def _mla_prep_kernel(cq_ref, ckv_ref, kr_ref, pos_ref, invf_ref, qn_ref, kvn_ref, wq_ref, wk_ref, wv_ref,
                     qt_ref, k_ref, vt_ref, cqn_scr, ckvn_scr, tab_scr, kpe_scr):
    def rope_t(pair):
        t = pair * tab_scr[...]
        return t[:QK_ROPE] + t[QK_ROPE:]

    @pl.when(pl.program_id(1) == 0)
    def _():
        cqn_scr[...] = _rms(cq_ref[...].astype(F32), qn_ref[...]).astype(BF16)
        ckvn_scr[...] = _rms(ckv_ref[...].astype(F32), kvn_ref[...]).astype(BF16)
        ang = invf_ref[...] * pos_ref[...].astype(F32)
        cos, sin = jnp.cos(ang), jnp.sin(ang)
        tab_scr[...] = jnp.concatenate([cos, cos, -sin, sin], axis=0)
        kpe_t = rope_t(kr_ref[...])
        kpe_scr[...] = jnp.concatenate([kpe_t, jnp.zeros_like(kpe_t)], axis=0).T.astype(BF16)

    scale = QK_DIM ** -0.5 * LOG2_E
    ckvn = ckvn_scr[...]
    qo = _dot_nt(wq_ref[...], cqn_scr[...])
    kn = _dot(ckvn, wk_ref[...]).astype(BF16)
    vo = _dot_nt(wv_ref[...], ckvn).astype(BF16)
    pad_row = lax.broadcasted_iota(jnp.int32, (VT_ROWS - V_DIM, vo.shape[1]), 0)
    ones_rows = jnp.where(pad_row == 0, 1.0, 0.0).astype(BF16)
    for hh in range(qt_ref.shape[1]):
        qh = qo[hh * QK_PAD:(hh + 1) * QK_PAD]
        qt_ref[0, hh, :QK_NOPE, :] = (qh[:QK_NOPE] * scale).astype(BF16)
        q_pe = (rope_t(qh[QK_NOPE:]) * scale).astype(BF16)
        qt_ref[0, hh, QK_NOPE:QK_DIM, :] = q_pe
        qt_ref[0, hh, QK_DIM:, :] = jnp.zeros_like(q_pe)
        k_ref[0, hh, :, :QK_NOPE] = kn[:, hh * QK_NOPE:(hh + 1) * QK_NOPE]
        k_ref[0, hh, :, QK_NOPE:] = kpe_scr[...]
        vt_ref[0, hh, :V_DIM, :] = vo[hh * V_DIM:(hh + 1) * V_DIM]
        vt_ref[0, hh, V_DIM:, :] = ones_rows


def _mla_prep(proj, small_t, pos, invf, q_norm, kv_norm, wq_t, wk, wv_t, batch, seq, tm, hb):
    t = proj.shape[0]
    spb = seq // tm
    h = N_HEADS_MLA
    return pl.pallas_call(
        _mla_prep_kernel,
        grid=(t // tm, h // hb),
        in_specs=[
            pl.BlockSpec((tm, Q_LORA), lambda m, hh: (m, 0)),
            pl.BlockSpec((tm, KV_LORA), lambda m, hh: (m, 1)),
            pl.BlockSpec((2 * QK_ROPE, tm), lambda m, hh: (0, m)),
            pl.BlockSpec((1, tm), lambda m, hh: (0, m)),
            pl.BlockSpec((QK_ROPE // 2, 1), lambda m, hh: (0, 0)),
            pl.BlockSpec((1, Q_LORA), lambda m, hh: (0, 0)),
            pl.BlockSpec((1, KV_LORA), lambda m, hh: (0, 0)),
            pl.BlockSpec((hb * QK_PAD, Q_LORA), lambda m, hh: (hh, 0)),
            pl.BlockSpec((KV_LORA, hb * QK_NOPE), lambda m, hh: (0, hh)),
            pl.BlockSpec((hb * V_DIM, KV_LORA), lambda m, hh: (hh, 0)),
        ],
        out_specs=[
            pl.BlockSpec((1, hb, QK_PAD, tm), lambda m, hh: (m // spb, hh, 0, m % spb)),
            pl.BlockSpec((1, hb, tm, QK_PAD), lambda m, hh: (m // spb, hh, m % spb, 0)),
            pl.BlockSpec((1, hb, VT_ROWS, tm), lambda m, hh: (m // spb, hh, 0, m % spb)),
        ],
        out_shape=[
            jax.ShapeDtypeStruct((batch, h, QK_PAD, seq), BF16),
            jax.ShapeDtypeStruct((batch, h, seq, QK_PAD), BF16),
            jax.ShapeDtypeStruct((batch, h, VT_ROWS, seq), BF16),
        ],
        scratch_shapes=[
            pltpu.VMEM((tm, Q_LORA), BF16),
            pltpu.VMEM((tm, KV_LORA), BF16),
            pltpu.VMEM((2 * QK_ROPE, tm), F32),
            pltpu.VMEM((tm, 2 * QK_ROPE), BF16),
        ],
        compiler_params=_params("parallel", "arbitrary"),
        name="mla_prep",
    )(proj, proj, small_t, pos, invf, q_norm, kv_norm, wq_t, wk, wv_t)


def _attn_kernel(qt_ref, k_ref, vt_ref, o_ref, m_scr, acc_scr, sa_scr, sb_scr, *, tq):
    qi = pl.program_id(2)
    m_scr[...] = jnp.full(m_scr.shape, -jnp.inf, F32)
    acc_scr[...] = jnp.zeros(acc_scr.shape, F32)

    def scores(ki):
        start = pl.multiple_of(ki * tq, tq)
        return _dot(k_ref[0, 0, pl.ds(start, tq), :], qt_ref[0, 0])

    def accumulate(ki, s):
        start = pl.multiple_of(ki * tq, tq)
        m_prev = m_scr[...]
        m_new = jnp.maximum(m_prev, jnp.max(s, axis=0, keepdims=True))
        p = jnp.exp2(s - m_new).astype(BF16)
        acc_scr[...] = jnp.exp2(m_prev - m_new) * acc_scr[...] + _dot(vt_ref[0, 0, :, pl.ds(start, tq)], p)
        m_scr[...] = m_new

    def accumulate_diagonal(s):
        key = lax.broadcasted_iota(jnp.int32, s.shape, 0)
        qry = lax.broadcasted_iota(jnp.int32, s.shape, 1)
        accumulate(qi, jnp.where(key <= qry, s, -jnp.inf))

    sa_scr[...] = scores(0)

    def pair(j, carry):
        sb_scr[...] = scores(2 * j + 1)
        accumulate(2 * j, sa_scr[...])
        sa_scr[...] = scores(2 * j + 2)
        accumulate(2 * j + 1, sb_scr[...])
        return carry

    lax.fori_loop(0, qi // 2, pair, 0)

    @pl.when(qi % 2 == 0)
    def _():
        accumulate_diagonal(sa_scr[...])

    @pl.when(qi % 2 == 1)
    def _():
        sb_scr[...] = scores(qi)
        accumulate(qi - 1, sa_scr[...])
        accumulate_diagonal(sb_scr[...])

    acc = acc_scr[...]
    o_ref[0] = (acc[:V_DIM] / acc[V_DIM:V_DIM + 1]).T.astype(BF16)


def _attention(qt, k, vt, tq):
    b, h, s, _ = k.shape
    return pl.pallas_call(
        functools.partial(_attn_kernel, tq=tq),
        grid=(b, h, s // tq),
        in_specs=[
            pl.BlockSpec((1, 1, QK_PAD, tq), lambda bb, hh, qi: (bb, hh, 0, qi)),
            pl.BlockSpec((1, 1, s, QK_PAD), lambda bb, hh, qi: (bb, hh, 0, 0)),
            pl.BlockSpec((1, 1, VT_ROWS, s), lambda bb, hh, qi: (bb, hh, 0, 0)),
        ],
        out_specs=pl.BlockSpec((1, tq, V_DIM), lambda bb, hh, qi: (bb, qi, hh)),
        out_shape=jax.ShapeDtypeStruct((b, s, h * V_DIM), BF16),
        scratch_shapes=[
            pltpu.VMEM((1, tq), F32),
            pltpu.VMEM((VT_ROWS, tq), F32),
            pltpu.VMEM((tq, tq), F32),
            pltpu.VMEM((tq, tq), F32),
        ],
        compiler_params=_params("parallel", "parallel", "arbitrary"),
        name="attention",
    )(qt, k, vt)


def _split3(x):
    hi = x.astype(BF16).astype(F32)
    r = x - hi
    mid = r.astype(BF16).astype(F32)
    return hi, mid, r - mid


_SSD_SLOT = (("xdt", (CHUNK, GROUP_WIDTH), BF16),
             ("xw", (CHUNK, GROUP_WIDTH), BF16),
             ("ecum", (CHUNK, GROUP_WIDTH), F32),
             ("dsx", (CHUNK, GROUP_WIDTH), F32),
             ("dlast", (1, GROUP_WIDTH), F32),
             ("cc", (CHUNK, D_STATE), BF16),
             ("bc", (CHUNK, D_STATE), BF16),
             ("bct", (D_STATE, CHUNK), BF16),
             ("cum", (HEADS_PER_GROUP, CHUNK), F32),
             ("cols", (CHUNK, 128), F32))


def _ssd_kernel(*refs):
    n_fixed = 21
    (x0_ref, b0_ref, c0_ref, dt0_ref, x1_ref, b1_ref, c1_ref, dt1_ref, x2_ref, b2_ref, c2_ref, dt2_ref,
     z_ref, gp_ref, hp_ref, expand_ref, tri_ref, shift_ref, o_ref, state_scr, tail_scr) = refs[:n_fixed]
    names = [n for n, _, _ in _SSD_SLOT]
    slots = [dict(zip(names, refs[n_fixed + i * len(names):n_fixed + (i + 1) * len(names)])) for i in range(2)]
    L = CHUNK
    gp = gp_ref[0]
    cw = gp[:CONV_WIDTH]
    bias = gp[CONV_WIDTH:CONV_WIDTH + 1]
    dskip = gp[CONV_WIDTH + 1:CONV_WIDTH + 2, :GROUP_WIDTH]
    norm_w = gp[CONV_WIDTH + 2:CONV_WIDTH + 3, :GROUP_WIDTH]
    hp = hp_ref[0]

    def stage_a(x_ref, b_ref, c_ref, dt_ref, out):
        raw = jnp.concatenate([x_ref[...], b_ref[...], c_ref[...]], axis=1)
        shifted = _dot(shift_ref[...], raw)
        rawf = raw.astype(F32)
        conv = bias + cw[CONV_WIDTH - 1:CONV_WIDTH, :] * rawf
        for k in range(1, CONV_WIDTH):
            conv = conv + cw[CONV_WIDTH - 1 - k:CONV_WIDTH - k, :] * shifted[(k - 1) * L:k * L]
        ext = jnp.concatenate([tail_scr[...], rawf[:CONV_TAIL]], axis=0)
        head = bias
        for tap in range(CONV_WIDTH):
            off = CONV_TAIL - (CONV_WIDTH - 1) + tap
            head = head + cw[tap:tap + 1, :] * ext[off:off + CONV_TAIL]
        conv = jnp.concatenate([head, conv[CONV_TAIL:]], axis=0)
        tail_scr[...] = rawf[L - CONV_TAIL:, :]
        act = _silu(conv)
        xc = act[:, :GROUP_WIDTH]
        bc = act[:, GROUP_WIDTH:GROUP_WIDTH + D_STATE]
        cc = act[:, GROUP_WIDTH + D_STATE:]

        dtr = dt_ref[...] + hp[:, 0:1]
        dt = jnp.maximum(dtr, 0.0) + jnp.log1p(jnp.exp(-jnp.abs(dtr)))
        da = dt * (-jnp.exp(hp[:, 1:2])) * LOG2_E
        d_hi, d_mid, d_lo = _split3(da)
        tri = tri_ref[...]
        cum = _dot(d_hi.astype(BF16), tri) + _dot(d_mid.astype(BF16), tri) + _dot(d_lo.astype(BF16), tri)

        t_hi, t_mid, t_lo = _split3(dt)
        c_hi, c_mid, c_lo = _split3(cum)
        stacked = jnp.concatenate(
            [t_hi, t_mid, t_lo, c_hi, c_mid, c_lo, cum, jnp.zeros((128 - 7 * HEADS_PER_GROUP, L), F32)], axis=0)
        cols = stacked.T
        expanded = _dot(cols.astype(BF16), expand_ref[...])
        dt_e = expanded[:, :GROUP_WIDTH]
        cum_e = expanded[:, GROUP_WIDTH:]
        last_e = cum_e[L - 1:L, :]
        xdt = xc * dt_e
        out["xdt"][...] = xdt.astype(BF16)
        out["xw"][...] = (xdt * jnp.exp2(last_e - cum_e)).astype(BF16)
        out["ecum"][...] = jnp.exp2(cum_e)
        out["dsx"][...] = dskip * xc
        out["dlast"][...] = jnp.exp2(last_e)
        out["cc"][...] = cc.astype(BF16)
        out["bc"][...] = bc.astype(BF16)
        out["bct"][...] = bc.T.astype(BF16)
        out["cum"][...] = cum
        out["cols"][...] = cols

    def stage_b(inp, rows):
        state = state_scr[...]
        cc_b = inp["cc"][...]
        y = _dot(cc_b, state.astype(BF16)) * inp["ecum"][...] + inp["dsx"][...]
        state_scr[...] = state * inp["dlast"][...] + _dot(inp["bct"][...], inp["xw"][...])

        cb = _dot_nt(cc_b, inp["bc"][...])
        ri = lax.broadcasted_iota(jnp.int32, (L, L), 0)
        ci = lax.broadcasted_iota(jnp.int32, (L, L), 1)
        causal = ci <= ri
        lane = lax.broadcasted_iota(jnp.int32, (L, 128), 1)
        cum = inp["cum"][...]
        cols = inp["cols"][...]
        pairs = []
        for pr in range(HEADS_PER_GROUP // 2):
            xp = inp["xdt"][:, pr * 128:(pr + 1) * 128]
            acc = None
            for sub in range(2):
                r = 2 * pr + sub
                col = cols[:, 6 * HEADS_PER_GROUP + r:6 * HEADS_PER_GROUP + r + 1]
                seg = col - cum[r:r + 1, :]
                m = (cb * jnp.exp2(jnp.where(causal, seg, -jnp.inf))).astype(BF16)
                keep = (lane < SSM_HEADDIM) if sub == 0 else (lane >= SSM_HEADDIM)
                part = _dot(m, jnp.where(keep, xp, jnp.zeros_like(xp)))
                acc = part if acc is None else acc + part
            pairs.append(acc)
        y = y + jnp.concatenate(pairs, axis=1)
        y = y * _silu(z_ref[rows, :].astype(F32))
        o_ref[rows, :] = _rms(y, norm_w).astype(BF16)

    @pl.when(pl.program_id(2) == 0)
    def _():
        state_scr[...] = jnp.zeros(state_scr.shape, F32)
        tail_scr[...] = jnp.zeros(tail_scr.shape, F32)
        stage_a(x0_ref, b0_ref, c0_ref, dt0_ref, slots[0])

    stage_a(x1_ref, b1_ref, c1_ref, dt1_ref, slots[1])
    stage_b(slots[0], slice(0, L))
    stage_a(x2_ref, b2_ref, c2_ref, dt2_ref, slots[0])
    stage_b(slots[1], slice(L, 2 * L))


def _shift_matrices():
    L = CHUNK
    shift = np.zeros(((CONV_WIDTH - 1) * L, L), np.float32)
    for k in range(1, CONV_WIDTH):
        for t in range(k, L):
            shift[(k - 1) * L + t, t - k] = 1.0
    return shift


def _ssd_group_params(conv_w, conv_b, d_skip, ssm_norm, dt_bias, a_log):
    g, gw, n = SSM_GROUPS, GROUP_WIDTH, D_STATE
    d_inner = g * gw
    per_group = lambda v: jnp.concatenate(
        [v[:, :d_inner].reshape(-1, g, gw), v[:, d_inner:d_inner + g * n].reshape(-1, g, n),
         v[:, d_inner + g * n:].reshape(-1, g, n)], axis=2).transpose(1, 0, 2)
    pad = lambda v: jnp.pad(v.reshape(g, 1, gw), ((0, 0), (0, 0), (0, 2 * n)))
    gp = jnp.concatenate([per_group(conv_w.astype(F32)), per_group(conv_b.astype(F32).reshape(1, -1)),
                          pad(jnp.repeat(d_skip.astype(F32), SSM_HEADDIM)), pad(ssm_norm.astype(F32)),
                          jnp.zeros((g, 8 - CONV_WIDTH - 3, gw + 2 * n), F32)], axis=1)
    hp = jnp.stack([dt_bias.astype(F32).reshape(g, HEADS_PER_GROUP), a_log.astype(F32).reshape(g, HEADS_PER_GROUP)],
                   axis=2)
    hp = jnp.pad(hp, ((0, 0), (0, 0), (0, 128 - hp.shape[2])))
    return gp, hp


def _ssd(proj, dt_t, gp, hp, expand, tri, shift, batch, seq, offs):
    t = proj.shape[0]
    L = CHUNK
    nc = seq // L
    g = SSM_GROUPS
    d_inner = g * GROUP_WIDTH
    row = lambda bb, gg, c: bb * nc + c
    z_blk, x_blk, b_blk, c_blk, dt_blk = offs
    def chunk_specs(chunk_of_step):
        r = lambda bb, gg, s: row(bb, gg, chunk_of_step(s))
        return [
            pl.BlockSpec((L, GROUP_WIDTH), lambda bb, gg, s: (r(bb, gg, s), x_blk + gg)),
            pl.BlockSpec((L, D_STATE), lambda bb, gg, s: (r(bb, gg, s), b_blk + gg)),
            pl.BlockSpec((L, D_STATE), lambda bb, gg, s: (r(bb, gg, s), c_blk + gg)),
            pl.BlockSpec((HEADS_PER_GROUP, L), lambda bb, gg, s: (dt_blk + gg, r(bb, gg, s))),
        ]

    pair_row = lambda bb, gg, s: bb * (nc // 2) + s
    chunk_inputs = (proj, proj, proj, dt_t)
    return pl.pallas_call(
        _ssd_kernel,
        grid=(batch, g, nc // 2),
        in_specs=[
            *chunk_specs(lambda s: 0),
            *chunk_specs(lambda s: 2 * s + 1),
            *chunk_specs(lambda s: jnp.minimum(2 * s + 2, nc - 1)),
            pl.BlockSpec((2 * L, GROUP_WIDTH), lambda bb, gg, s: (pair_row(bb, gg, s), z_blk + gg)),
            pl.BlockSpec((1,) + gp.shape[1:], lambda bb, gg, s: (gg, 0, 0)),
            pl.BlockSpec((1,) + hp.shape[1:], lambda bb, gg, s: (gg, 0, 0)),
            pl.BlockSpec((128, 2 * GROUP_WIDTH), lambda bb, gg, s: (0, 0)),
            pl.BlockSpec((L, L), lambda bb, gg, s: (0, 0)),
            pl.BlockSpec(((CONV_WIDTH - 1) * L, L), lambda bb, gg, s: (0, 0)),
        ],
        out_specs=pl.BlockSpec((2 * L, GROUP_WIDTH), lambda bb, gg, s: (pair_row(bb, gg, s), gg)),
        out_shape=jax.ShapeDtypeStruct((t, d_inner), BF16),
        scratch_shapes=[
            pltpu.VMEM((D_STATE, GROUP_WIDTH), F32),
            pltpu.VMEM((CONV_TAIL, GROUP_WIDTH + 2 * D_STATE), F32),
            *[pltpu.VMEM(shape, dtype) for _ in range(2) for _, shape, dtype in _SSD_SLOT],
        ],
        compiler_params=_params("parallel", "parallel", "arbitrary"),
        name="ssd",
    )(*chunk_inputs, *chunk_inputs, *chunk_inputs, proj, gp, hp, expand, tri, shift)


def _mix_kernel(attn_ref, ssm_ref, ga_ref, gs_ref, wa_ref, ws_ref, o_ref):
    a = _dot(attn_ref[...], wa_ref[...])
    s = _dot(ssm_ref[...], ws_ref[...])
    mixed = _sigmoid(ga_ref[...].astype(F32)) * a + _sigmoid(gs_ref[...].astype(F32)) * s
    o_ref[...] = mixed.astype(BF16)


def _mix(attn, ssm, proj, wa, ws, ga_blk, gs_blk, tm, tn):
    t = attn.shape[0]
    d = wa.shape[1]
    return pl.pallas_call(
        _mix_kernel,
        grid=(t // tm, d // tn),
        in_specs=[
            pl.BlockSpec((tm, attn.shape[1]), lambda m, n: (m, 0)),
            pl.BlockSpec((tm, ssm.shape[1]), lambda m, n: (m, 0)),
            pl.BlockSpec((tm, tn), lambda m, n: (m, ga_blk + n)),
            pl.BlockSpec((tm, tn), lambda m, n: (m, gs_blk + n)),
            pl.BlockSpec((wa.shape[0], tn), lambda m, n: (0, n)),
            pl.BlockSpec((ws.shape[0], tn), lambda m, n: (0, n)),
        ],
        out_specs=pl.BlockSpec((tm, tn), lambda m, n: (m, n)),
        out_shape=jax.ShapeDtypeStruct((t, d), BF16),
        compiler_params=_params("parallel", "arbitrary"),
        name="mix",
    )(attn, ssm, proj, proj, wa, ws)


def _mix_out_kernel(mixed_ref, x_ref, w_ref, post_ref, pre_ref, h_ref, f_ref):
    groups = _row_groups(x_ref.shape[0], ROW_GROUPS)
    outs = [_dot(mixed_ref[r, :], w_ref[...]) for r in groups]
    for r, o in zip(groups, outs):
        h = x_ref[r, :] + _rms(o, post_ref[...])
        h_ref[r, :] = h
        f_ref[r, :] = _rms(h, pre_ref[...]).astype(BF16)


def _mix_out(mixed, x2, w_out, post_w, pre_w, tm):
    t, d = x2.shape
    row = pl.BlockSpec((tm, d), lambda m: (m, 0))
    vec = pl.BlockSpec((1, d), lambda m: (0, 0))
    return pl.pallas_call(
        _mix_out_kernel,
        grid=(t // tm,),
        in_specs=[row, row, pl.BlockSpec((d, d), lambda m: (0, 0)), vec, vec],
        out_specs=[row, row],
        out_shape=[jax.ShapeDtypeStruct((t, d), F32), jax.ShapeDtypeStruct((t, d), BF16)],
        compiler_params=_params("parallel"),
        name="mix_out",
    )(mixed, x2, w_out, post_w, pre_w)


def _ffn_kernel(f_ref, wg_ref, wu_ref, wd_ref, o_ref, acc_scr):
    kf = pl.program_id(1)

    @pl.when(kf == 0)
    def _():
        acc_scr[...] = jnp.zeros(acc_scr.shape, F32)

    f = f_ref[...]
    g = _dot(f, wg_ref[...].astype(BF16))
    u = _dot(f, wu_ref[...].astype(BF16))
    a = (_silu(g) * u).astype(BF16)
    acc_scr[...] += _dot(a, wd_ref[...].astype(BF16))

    @pl.when(kf == pl.num_programs(1) - 1)
    def _():
        o_ref[...] = acc_scr[...].astype(BF16)


def _ffn(f, wg, wu, wd, tm, tf):
    t, d = f.shape
    dff = wg.shape[1]
    return pl.pallas_call(
        _ffn_kernel,
        grid=(t // tm, dff // tf),
        in_specs=[
            pl.BlockSpec((tm, d), lambda m, k: (m, 0)),
            pl.BlockSpec((d, tf), lambda m, k: (0, k)),
            pl.BlockSpec((d, tf), lambda m, k: (0, k)),
            pl.BlockSpec((tf, d), lambda m, k: (k, 0)),
        ],
        out_specs=pl.BlockSpec((tm, d), lambda m, k: (m, 0)),
        out_shape=jax.ShapeDtypeStruct((t, d), BF16),
        scratch_shapes=[pltpu.VMEM((tm, d), F32)],
        compiler_params=_params("parallel", "arbitrary"),
        name="ffn",
    )(f, wg, wu, wd)


def _ple_kernel(h_ref, d_ref, p_ref, fpost_ref, ppre_ref, ppost_ref, wg_ref, wp_ref, o_ref):
    groups = _row_groups(h_ref.shape[0], ROW_GROUPS)
    hs = [h_ref[r, :] + _rms(d_ref[r, :].astype(F32), fpost_ref[...]) for r in groups]
    gates = [_dot(_rms(h, ppre_ref[...]).astype(BF16), wg_ref[...]) for h in hs]
    es = [_dot(p_ref[r, :].astype(BF16), wp_ref[...]) * _sigmoid(g) for r, g in zip(groups, gates)]
    for r, h, e in zip(groups, hs, es):
        o_ref[r, :] = h + _rms(e, ppost_ref[...])


def _ple(h1, dff, p2, fpost, ppre, ppost, wg, wp, tm):
    t, d = h1.shape
    row = pl.BlockSpec((tm, d), lambda m: (m, 0))
    vec = pl.BlockSpec((1, d), lambda m: (0, 0))
    return pl.pallas_call(
        _ple_kernel,
        grid=(t // tm,),
        in_specs=[row, row, pl.BlockSpec((tm, p2.shape[1]), lambda m: (m, 0)), vec, vec, vec,
                  pl.BlockSpec((d, d), lambda m: (0, 0)),
                  pl.BlockSpec((p2.shape[1], d), lambda m: (0, 0))],
        out_specs=row,
        out_shape=jax.ShapeDtypeStruct((t, d), F32),
        compiler_params=_params("parallel"),
        name="ple",
    )(h1, dff, p2, fpost, ppre, ppost, wg, wp)


def _expand_matrix():
    e = np.zeros((128, 2 * GROUP_WIDTH), np.float32)
    for part in range(3):
        for r in range(HEADS_PER_GROUP):
            lo = r * SSM_HEADDIM
            e[part * HEADS_PER_GROUP + r, lo:lo + SSM_HEADDIM] = 1.0
            e[(3 + part) * HEADS_PER_GROUP + r, GROUP_WIDTH + lo:GROUP_WIDTH + lo + SSM_HEADDIM] = 1.0
    return e


def _layer(h, p_i, pos, invf, mix_norm_pre, mix_norm_post, w_in, q_norm, w_uq, kv_norm, w_ukv,
           conv_w, conv_b, dt_bias, a_log, d_skip, ssm_norm, w_attn_o, w_ssm_o, w_out,
           ffn_norm_pre, ffn_norm_post, w_gate, w_up, w_down,
           ple_norm_pre, ple_norm_post, w_ple_gate, w_ple):
    batch, seq, d = h.shape
    t = batch * seq
    d_inner = SSM_GROUPS * GROUP_WIDTH
    bc_width = SSM_GROUPS * D_STATE
    n_ssm_heads = SSM_GROUPS * HEADS_PER_GROUP
    vec = lambda v: v.reshape(1, -1).astype(F32)

    o_kr = Q_LORA + KV_LORA
    o_z = o_kr + QK_ROPE
    o_xbc = o_z + d_inner
    o_dt = o_xbc + d_inner + 2 * bc_width
    o_ga = o_dt + n_ssm_heads
    w_t = w_in.T.astype(BF16)
    segments = ((0, o_kr), (o_z, o_dt - o_z), (o_ga, w_in.shape[1] - o_ga))
    half = QK_ROPE // 2
    w_small_t = jnp.concatenate([w_t[o_kr:o_z], w_t[o_kr + half:o_z], w_t[o_kr:o_kr + half], w_t[o_dt:o_ga]], axis=0)
    dt_blk = 2 * QK_ROPE // HEADS_PER_GROUP
    z_blk = (Q_LORA + KV_LORA) // GROUP_WIDTH
    x_blk = z_blk + SSM_GROUPS
    b_blk = (Q_LORA + KV_LORA + 2 * d_inner) // D_STATE
    c_blk = b_blk + SSM_GROUPS
    g_off = Q_LORA + KV_LORA + 2 * d_inner + 2 * bc_width

    x2 = h.reshape(t, d)
    proj, small_t = _inproj(x2, vec(mix_norm_pre), w_t, segments, w_small_t, tm=1024, tn=1024)

    wq = w_uq.reshape(Q_LORA, N_HEADS_MLA, QK_DIM)
    wq = jnp.concatenate([wq, wq[:, :, QK_NOPE + half:], wq[:, :, QK_NOPE:QK_NOPE + half]], axis=2)
    wq_t = jnp.transpose(wq, (1, 2, 0)).reshape(N_HEADS_MLA * QK_PAD, Q_LORA).astype(BF16)
    wkv = w_ukv.reshape(KV_LORA, N_HEADS_MLA, QK_NOPE + V_DIM)
    wk = wkv[:, :, :QK_NOPE].reshape(KV_LORA, N_HEADS_MLA * QK_NOPE).astype(BF16)
    wv_t = jnp.transpose(wkv[:, :, QK_NOPE:], (1, 2, 0)).reshape(N_HEADS_MLA * V_DIM, KV_LORA).astype(BF16)
    qt, k, vt = _mla_prep(proj, small_t, pos, invf, vec(q_norm), vec(kv_norm), wq_t, wk, wv_t,
                          batch, seq, tm=512, hb=4)
    attn = _attention(qt, k, vt, tq=1024).reshape(t, N_HEADS_MLA * V_DIM)

    expand = jnp.asarray(_expand_matrix(), BF16)
    tri = jnp.asarray(np.triu(np.ones((CHUNK, CHUNK), np.float32)), BF16)
    gp, hp = _ssd_group_params(conv_w, conv_b, d_skip, ssm_norm, dt_bias, a_log)
    ssm = _ssd(proj, small_t, gp, hp, expand, tri, jnp.asarray(_shift_matrices(), BF16), batch, seq,
               (z_blk, x_blk, b_blk, c_blk, dt_blk))

    tn_mix = 512
    mixed = _mix(attn, ssm, proj, w_attn_o.astype(BF16), w_ssm_o.astype(BF16),
                 g_off // tn_mix, (g_off + d) // tn_mix, tm=1024, tn=tn_mix)
    h1, f = _mix_out(mixed, x2, w_out.astype(BF16), vec(mix_norm_post), vec(ffn_norm_pre), tm=512)
    dff = _ffn(f, w_gate, w_up, w_down, tm=1024, tf=512)
    out = _ple(h1, dff, p_i.reshape(t, -1), vec(ffn_norm_post), vec(ple_norm_pre), vec(ple_norm_post),
               w_ple_gate.astype(BF16), w_ple.astype(BF16), tm=512)
    return out.reshape(batch, seq, d)


def kernel(x, p, positions, mix_norm_pre, mix_norm_post, w_in, q_norm, w_uq, kv_norm, w_ukv, conv_w, conv_b, dt_bias, a_log, d_skip, ssm_norm, w_attn_o, w_ssm_o, w_out, ffn_norm_pre, ffn_norm_post, w_gate, w_up, w_down, ple_norm_pre, ple_norm_post, w_ple_gate, w_ple):
    invf = ROPE_THETA ** (-jnp.arange(0, QK_ROPE, 2, dtype=F32) / QK_ROPE)
    invf = invf.reshape(QK_ROPE // 2, 1)
    pos = positions.reshape(1, -1)
    h = x
    for i in range(w_in.shape[0]):
        h = _layer(h, p[i], pos, invf, mix_norm_pre[i], mix_norm_post[i], w_in[i], q_norm[i], w_uq[i],
                   kv_norm[i], w_ukv[i], conv_w[i], conv_b[i], dt_bias[i], a_log[i], d_skip[i], ssm_norm[i],
                   w_attn_o[i], w_ssm_o[i], w_out[i], ffn_norm_pre[i], ffn_norm_post[i],
                   w_gate[i], w_up[i], w_down[i], ple_norm_pre[i], ple_norm_post[i], w_ple_gate[i], w_ple[i])
    return h
```

```python
import functools
import math

import jax
import jax.numpy as jnp
import numpy as np
from jax import lax
from jax.experimental import pallas as pl
from jax.experimental.pallas import tpu as pltpu

F32 = jnp.float32
BF16 = jnp.bfloat16

EPS = 1e-6
ROPE_THETA = 10000.0
LOG2_E = 1.4426950408889634

N_HEADS_MLA = 16
Q_LORA = 512
KV_LORA = 512
QK_NOPE = 128
QK_ROPE = 64
V_DIM = 128
QK_DIM = QK_NOPE + QK_ROPE
QK_PAD = 256
VT_ROWS = V_DIM + 16

SSM_HEADDIM = 64
SSM_GROUPS = 8
HEADS_PER_GROUP = 8
GROUP_WIDTH = HEADS_PER_GROUP * SSM_HEADDIM
D_STATE = 128
CONV_WIDTH = 4
CHUNK = 256
CONV_TAIL = 8

VMEM_LIMIT = 56 * 1024 * 1024


def _params(*sem):
    return pltpu.CompilerParams(dimension_semantics=sem, vmem_limit_bytes=VMEM_LIMIT)


def _rms(x, w):
    return x * lax.rsqrt(jnp.mean(x * x, axis=-1, keepdims=True) + EPS) * w


def _sigmoid(x):
    return 0.5 + 0.5 * jnp.tanh(0.5 * x)


def _silu(x):
    h = 0.5 * x
    return h + h * jnp.tanh(h)


def _dot(a, b):
    return jnp.dot(a, b, preferred_element_type=F32)


ROW_GROUPS = 4


def _row_groups(rows, n):
    step = rows // n
    return [slice(i * step, (i + 1) * step) for i in range(n)]


def _dot_nt(a, b):
    return lax.dot_general(a, b, (((1,), (1,)), ((), ())), preferred_element_type=F32)


def _inproj_kernel(x_ref, nw_ref, w_ref, wt_ref, o_ref, ot_ref, u_scr):
    @pl.when(pl.program_id(1) == 0)
    def _():
        u = _rms(x_ref[...], nw_ref[...]).astype(BF16)
        u_scr[...] = u
        ot_ref[...] = _dot_nt(wt_ref[...], u)

    o_ref[...] = _dot_nt(u_scr[...], w_ref[...]).astype(BF16)


def _inproj(x2, norm_w, w_t, segments, w_small_t, tm, tn):
    t, d = x2.shape
    bounds = [0]
    for _, count in segments:
        bounds.append(bounds[-1] + count // tn)
    n_main = bounds[-1] * tn
    n_small = w_small_t.shape[0]

    def w_row(n):
        row = segments[0][0] + n * tn
        for (start, _), lo in zip(segments[1:], bounds[1:]):
            row = jnp.where(n >= lo, start + (n - lo) * tn, row)
        return pl.multiple_of(row, math.gcd(tn, *[start for start, _ in segments]))

    return pl.pallas_call(
        _inproj_kernel,
        grid=(t // tm, n_main // tn),
        in_specs=[
            pl.BlockSpec((tm, d), lambda m, n: (m, 0)),
            pl.BlockSpec((1, d), lambda m, n: (0, 0)),
            pl.BlockSpec((pl.Element(tn), pl.Element(d)), lambda m, n: (w_row(n), 0)),
            pl.BlockSpec((n_small, d), lambda m, n: (0, 0)),
        ],
        out_specs=[
            pl.BlockSpec((tm, tn), lambda m, n: (m, n)),
            pl.BlockSpec((n_small, tm), lambda m, n: (0, m)),
        ],
        out_shape=[
            jax.ShapeDtypeStruct((t, n_main), BF16),
            jax.ShapeDtypeStruct((n_small, t), F32),
        ],
        scratch_shapes=[pltpu.VMEM((tm, d), BF16)],
        compiler_params=_params("parallel", "arbitrary"),
        name="inproj",
    )(x2, norm_w, w_t, w_small_t)


def _mla_prep_kernel(cq_ref, ckv_ref, kr_ref, pos_ref, invf_ref, qn_ref, kvn_ref, wq_ref, wk_ref, wv_ref,
                     qt_ref, k_ref, vt_ref, cqn_scr, ckvn_scr, tab_scr, kpe_scr):
    def rope_t(pair):
        t = pair * tab_scr[...]
        return t[:QK_ROPE] + t[QK_ROPE:]

    @pl.when(pl.program_id(1) == 0)
    def _():
        cqn_scr[...] = _rms(cq_ref[...].astype(F32), qn_ref[...]).astype(BF16)
        ckvn_scr[...] = _rms(ckv_ref[...].astype(F32), kvn_ref[...]).astype(BF16)
        ang = invf_ref[...] * pos_ref[...].astype(F32)
        cos, sin = jnp.cos(ang), jnp.sin(ang)
        tab_scr[...] = jnp.concatenate([cos, cos, -sin, sin], axis=0)
        kpe_t = rope_t(kr_ref[...])
        kpe_scr[...] = jnp.concatenate([kpe_t, jnp.zeros_like(kpe_t)], axis=0).T.astype(BF16)

    scale = QK_DIM ** -0.5 * LOG2_E
    ckvn = ckvn_scr[...]
    qo = _dot_nt(wq_ref[...], cqn_scr[...])
    kn = _dot(ckvn, wk_ref[...]).astype(BF16)
    vo = _dot_nt(wv_ref[...], ckvn).astype(BF16)
    pad_row = lax.broadcasted_iota(jnp.int32, (VT_ROWS - V_DIM, vo.shape[1]), 0)
    ones_rows = jnp.where(pad_row == 0, 1.0, 0.0).astype(BF16)
    for hh in range(qt_ref.shape[1]):
        qh = qo[hh * QK_PAD:(hh + 1) * QK_PAD]
        qt_ref[0, hh, :QK_NOPE, :] = (qh[:QK_NOPE] * scale).astype(BF16)
        q_pe = (rope_t(qh[QK_NOPE:]) * scale).astype(BF16)
        qt_ref[0, hh, QK_NOPE:QK_DIM, :] = q_pe
        qt_ref[0, hh, QK_DIM:, :] = jnp.zeros_like(q_pe)
        k_ref[0, hh, :, :QK_NOPE] = kn[:, hh * QK_NOPE:(hh + 1) * QK_NOPE]
        k_ref[0, hh, :, QK_NOPE:] = kpe_scr[...]
        vt_ref[0, hh, :V_DIM, :] = vo[hh * V_DIM:(hh + 1) * V_DIM]
        vt_ref[0, hh, V_DIM:, :] = ones_rows


def _mla_prep(proj, small_t, pos, invf, q_norm, kv_norm, wq_t, wk, wv_t, batch, seq, tm, hb):
    t = proj.shape[0]
    spb = seq // tm
    h = N_HEADS_MLA
    return pl.pallas_call(
        _mla_prep_kernel,
        grid=(t // tm, h // hb),
        in_specs=[
            pl.BlockSpec((tm, Q_LORA), lambda m, hh: (m, 0)),
            pl.BlockSpec((tm, KV_LORA), lambda m, hh: (m, 1)),
            pl.BlockSpec((2 * QK_ROPE, tm), lambda m, hh: (0, m)),
            pl.BlockSpec((1, tm), lambda m, hh: (0, m)),
            pl.BlockSpec((QK_ROPE // 2, 1), lambda m, hh: (0, 0)),
            pl.BlockSpec((1, Q_LORA), lambda m, hh: (0, 0)),
            pl.BlockSpec((1, KV_LORA), lambda m, hh: (0, 0)),
            pl.BlockSpec((hb * QK_PAD, Q_LORA), lambda m, hh: (hh, 0)),
            pl.BlockSpec((KV_LORA, hb * QK_NOPE), lambda m, hh: (0, hh)),
            pl.BlockSpec((hb * V_DIM, KV_LORA), lambda m, hh: (hh, 0)),
        ],
        out_specs=[
            pl.BlockSpec((1, hb, QK_PAD, tm), lambda m, hh: (m // spb, hh, 0, m % spb)),
            pl.BlockSpec((1, hb, tm, QK_PAD), lambda m, hh: (m // spb, hh, m % spb, 0)),
            pl.BlockSpec((1, hb, VT_ROWS, tm), lambda m, hh: (m // spb, hh, 0, m % spb)),
        ],
        out_shape=[
            jax.ShapeDtypeStruct((batch, h, QK_PAD, seq), BF16),
            jax.ShapeDtypeStruct((batch, h, seq, QK_PAD), BF16),
            jax.ShapeDtypeStruct((batch, h, VT_ROWS, seq), BF16),
        ],
        scratch_shapes=[
            pltpu.VMEM((tm, Q_LORA), BF16),
            pltpu.VMEM((tm, KV_LORA), BF16),
            pltpu.VMEM((2 * QK_ROPE, tm), F32),
            pltpu.VMEM((tm, 2 * QK_ROPE), BF16),
        ],
        compiler_params=_params("parallel", "arbitrary"),
        name="mla_prep",
    )(proj, proj, small_t, pos, invf, q_norm, kv_norm, wq_t, wk, wv_t)


def _attn_kernel(qt_ref, k_ref, vt_ref, o_ref, m_scr, acc_scr, sa_scr, sb_scr, *, tq):
    qi = pl.program_id(2)
    m_scr[...] = jnp.full(m_scr.shape, -jnp.inf, F32)
    acc_scr[...] = jnp.zeros(acc_scr.shape, F32)

    def scores(ki):
        start = pl.multiple_of(ki * tq, tq)
        return _dot(k_ref[0, 0, pl.ds(start, tq), :], qt_ref[0, 0])

    def accumulate(ki, s):
        start = pl.multiple_of(ki * tq, tq)
        m_prev = m_scr[...]
        m_new = jnp.maximum(m_prev, jnp.max(s, axis=0, keepdims=True))
        p = jnp.exp2(s - m_new).astype(BF16)
        acc_scr[...] = jnp.exp2(m_prev - m_new) * acc_scr[...] + _dot(vt_ref[0, 0, :, pl.ds(start, tq)], p)
        m_scr[...] = m_new

    def accumulate_diagonal(s):
        key = lax.broadcasted_iota(jnp.int32, s.shape, 0)
        qry = lax.broadcasted_iota(jnp.int32, s.shape, 1)
        accumulate(qi, jnp.where(key <= qry, s, -jnp.inf))

    sa_scr[...] = scores(0)

    def pair(j, carry):
        sb_scr[...] = scores(2 * j + 1)
        accumulate(2 * j, sa_scr[...])
        sa_scr[...] = scores(2 * j + 2)
        accumulate(2 * j + 1, sb_scr[...])
        return carry

    lax.fori_loop(0, qi // 2, pair, 0)

    @pl.when(qi % 2 == 0)
    def _():
        accumulate_diagonal(sa_scr[...])

    @pl.when(qi % 2 == 1)
    def _():
        sb_scr[...] = scores(qi)
        accumulate(qi - 1, sa_scr[...])
        accumulate_diagonal(sb_scr[...])

    acc = acc_scr[...]
    o_ref[0] = (acc[:V_DIM] / acc[V_DIM:V_DIM + 1]).T.astype(BF16)


def _attention(qt, k, vt, tq):
    b, h, s, _ = k.shape
    return pl.pallas_call(
        functools.partial(_attn_kernel, tq=tq),
        grid=(b, h, s // tq),
        in_specs=[
            pl.BlockSpec((1, 1, QK_PAD, tq), lambda bb, hh, qi: (bb, hh, 0, qi)),
            pl.BlockSpec((1, 1, s, QK_PAD), lambda bb, hh, qi: (bb, hh, 0, 0)),
            pl.BlockSpec((1, 1, VT_ROWS, s), lambda bb, hh, qi: (bb, hh, 0, 0)),
        ],
        out_specs=pl.BlockSpec((1, tq, V_DIM), lambda bb, hh, qi: (bb, qi, hh)),
        out_shape=jax.ShapeDtypeStruct((b, s, h * V_DIM), BF16),
        scratch_shapes=[
            pltpu.VMEM((1, tq), F32),
            pltpu.VMEM((VT_ROWS, tq), F32),
            pltpu.VMEM((tq, tq), F32),
            pltpu.VMEM((tq, tq), F32),
        ],
        compiler_params=_params("parallel", "parallel", "arbitrary"),
        name="attention",
    )(qt, k, vt)


def _split3(x):
    hi = x.astype(BF16).astype(F32)
    r = x - hi
    mid = r.astype(BF16).astype(F32)
    return hi, mid, r - mid


_SSD_SLOT = (("xdt", (CHUNK, GROUP_WIDTH), BF16),
             ("xw", (CHUNK, GROUP_WIDTH), BF16),
             ("ecum", (CHUNK, GROUP_WIDTH), F32),
             ("dsx", (CHUNK, GROUP_WIDTH), F32),
             ("dlast", (1, GROUP_WIDTH), F32),
             ("cc", (CHUNK, D_STATE), BF16),
             ("bc", (CHUNK, D_STATE), BF16),
             ("bct", (D_STATE, CHUNK), BF16),
             ("cum", (HEADS_PER_GROUP, CHUNK), F32),
             ("cols", (CHUNK, 128), F32))


def _ssd_kernel(*refs):
    n_fixed = 21
    (x0_ref, b0_ref, c0_ref, dt0_ref, x1_ref, b1_ref, c1_ref, dt1_ref, x2_ref, b2_ref, c2_ref, dt2_ref,
     z_ref, gp_ref, hp_ref, expand_ref, tri_ref, shift_ref, o_ref, state_scr, tail_scr) = refs[:n_fixed]
    names = [n for n, _, _ in _SSD_SLOT]
    slots = [dict(zip(names, refs[n_fixed + i * len(names):n_fixed + (i + 1) * len(names)])) for i in range(2)]
    L = CHUNK
    gp = gp_ref[0]
    cw = gp[:CONV_WIDTH]
    bias = gp[CONV_WIDTH:CONV_WIDTH + 1]
    dskip = gp[CONV_WIDTH + 1:CONV_WIDTH + 2, :GROUP_WIDTH]
    norm_w = gp[CONV_WIDTH + 2:CONV_WIDTH + 3, :GROUP_WIDTH]
    hp = hp_ref[0]

    def stage_a(x_ref, b_ref, c_ref, dt_ref, out):
        raw = jnp.concatenate([x_ref[...], b_ref[...], c_ref[...]], axis=1)
        shifted = _dot(shift_ref[...], raw)
        rawf = raw.astype(F32)
        conv = bias + cw[CONV_WIDTH - 1:CONV_WIDTH, :] * rawf
        for k in range(1, CONV_WIDTH):
            conv = conv + cw[CONV_WIDTH - 1 - k:CONV_WIDTH - k, :] * shifted[(k - 1) * L:k * L]
        ext = jnp.concatenate([tail_scr[...], rawf[:CONV_TAIL]], axis=0)
        head = bias
        for tap in range(CONV_WIDTH):
            off = CONV_TAIL - (CONV_WIDTH - 1) + tap
            head = head + cw[tap:tap + 1, :] * ext[off:off + CONV_TAIL]
        conv = jnp.concatenate([head, conv[CONV_TAIL:]], axis=0)
        tail_scr[...] = rawf[L - CONV_TAIL:, :]
        act = _silu(conv)
        xc = act[:, :GROUP_WIDTH]
        bc = act[:, GROUP_WIDTH:GROUP_WIDTH + D_STATE]
        cc = act[:, GROUP_WIDTH + D_STATE:]

        dtr = dt_ref[...] + hp[:, 0:1]
        dt = jnp.maximum(dtr, 0.0) + jnp.log1p(jnp.exp(-jnp.abs(dtr)))
        da = dt * (-jnp.exp(hp[:, 1:2])) * LOG2_E
        d_hi, d_mid, d_lo = _split3(da)
        tri = tri_ref[...]
        cum = _dot(d_hi.astype(BF16), tri) + _dot(d_mid.astype(BF16), tri) + _dot(d_lo.astype(BF16), tri)

        t_hi, t_mid, t_lo = _split3(dt)
        c_hi, c_mid, c_lo = _split3(cum)
        stacked = jnp.concatenate(
            [t_hi, t_mid, t_lo, c_hi, c_mid, c_lo, cum, jnp.zeros((128 - 7 * HEADS_PER_GROUP, L), F32)], axis=0)
        cols = stacked.T
        expanded = _dot(cols.astype(BF16), expand_ref[...])
        dt_e = expanded[:, :GROUP_WIDTH]
        cum_e = expanded[:, GROUP_WIDTH:]
        last_e = cum_e[L - 1:L, :]
        xdt = xc * dt_e
        out["xdt"][...] = xdt.astype(BF16)
        out["xw"][...] = (xdt * jnp.exp2(last_e - cum_e)).astype(BF16)
        out["ecum"][...] = jnp.exp2(cum_e)
        out["dsx"][...] = dskip * xc
        out["dlast"][...] = jnp.exp2(last_e)
        out["cc"][...] = cc.astype(BF16)
        out["bc"][...] = bc.astype(BF16)
        out["bct"][...] = bc.T.astype(BF16)
        out["cum"][...] = cum
        out["cols"][...] = cols

    def stage_b(inp, rows):
        state = state_scr[...]
        cc_b = inp["cc"][...]
        y = _dot(cc_b, state.astype(BF16)) * inp["ecum"][...] + inp["dsx"][...]
        state_scr[...] = state * inp["dlast"][...] + _dot(inp["bct"][...], inp["xw"][...])

        cb = _dot_nt(cc_b, inp["bc"][...])
        ri = lax.broadcasted_iota(jnp.int32, (L, L), 0)
        ci = lax.broadcasted_iota(jnp.int32, (L, L), 1)
        causal = ci <= ri
        lane = lax.broadcasted_iota(jnp.int32, (L, 128), 1)
        cum = inp["cum"][...]
        cols = inp["cols"][...]
        pairs = []
        for pr in range(HEADS_PER_GROUP // 2):
            xp = inp["xdt"][:, pr * 128:(pr + 1) * 128]
            acc = None
            for sub in range(2):
                r = 2 * pr + sub
                col = cols[:, 6 * HEADS_PER_GROUP + r:6 * HEADS_PER_GROUP + r + 1]
                seg = col - cum[r:r + 1, :]
                m = (cb * jnp.exp2(jnp.where(causal, seg, -jnp.inf))).astype(BF16)
                keep = (lane < SSM_HEADDIM) if sub == 0 else (lane >= SSM_HEADDIM)
                part = _dot(m, jnp.where(keep, xp, jnp.zeros_like(xp)))
                acc = part if acc is None else acc + part
            pairs.append(acc)
        y = y + jnp.concatenate(pairs, axis=1)
        y = y * _silu(z_ref[rows, :].astype(F32))
        o_ref[rows, :] = _rms(y, norm_w).astype(BF16)

    @pl.when(pl.program_id(2) == 0)
    def _():
        state_scr[...] = jnp.zeros(state_scr.shape, F32)
        tail_scr[...] = jnp.zeros(tail_scr.shape, F32)
        stage_a(x0_ref, b0_ref, c0_ref, dt0_ref, slots[0])

    stage_a(x1_ref, b1_ref, c1_ref, dt1_ref, slots[1])
    stage_b(slots[0], slice(0, L))
    stage_a(x2_ref, b2_ref, c2_ref, dt2_ref, slots[0])
    stage_b(slots[1], slice(L, 2 * L))


def _shift_matrices():
    L = CHUNK
    shift = np.zeros(((CONV_WIDTH - 1) * L, L), np.float32)
    for k in range(1, CONV_WIDTH):
        for t in range(k, L):
            shift[(k - 1) * L + t, t - k] = 1.0
    return shift


def _ssd_group_params(conv_w, conv_b, d_skip, ssm_norm, dt_bias, a_log):
    g, gw, n = SSM_GROUPS, GROUP_WIDTH, D_STATE
    d_inner = g * gw
    per_group = lambda v: jnp.concatenate(
        [v[:, :d_inner].reshape(-1, g, gw), v[:, d_inner:d_inner + g * n].reshape(-1, g, n),
         v[:, d_inner + g * n:].reshape(-1, g, n)], axis=2).transpose(1, 0, 2)
    pad = lambda v: jnp.pad(v.reshape(g, 1, gw), ((0, 0), (0, 0), (0, 2 * n)))
    gp = jnp.concatenate([per_group(conv_w.astype(F32)), per_group(conv_b.astype(F32).reshape(1, -1)),
                          pad(jnp.repeat(d_skip.astype(F32), SSM_HEADDIM)), pad(ssm_norm.astype(F32)),
                          jnp.zeros((g, 8 - CONV_WIDTH - 3, gw + 2 * n), F32)], axis=1)
    hp = jnp.stack([dt_bias.astype(F32).reshape(g, HEADS_PER_GROUP), a_log.astype(F32).reshape(g, HEADS_PER_GROUP)],
                   axis=2)
    hp = jnp.pad(hp, ((0, 0), (0, 0), (0, 128 - hp.shape[2])))
    return gp, hp


def _ssd(proj, dt_t, gp, hp, expand, tri, shift, batch, seq, offs):
    t = proj.shape[0]
    L = CHUNK
    nc = seq // L
    g = SSM_GROUPS
    d_inner = g * GROUP_WIDTH
    row = lambda bb, gg, c: bb * nc + c
    z_blk, x_blk, b_blk, c_blk, dt_blk = offs
    def chunk_specs(chunk_of_step):
        r = lambda bb, gg, s: row(bb, gg, chunk_of_step(s))
        return [
            pl.BlockSpec((L, GROUP_WIDTH), lambda bb, gg, s: (r(bb, gg, s), x_blk + gg)),
            pl.BlockSpec((L, D_STATE), lambda bb, gg, s: (r(bb, gg, s), b_blk + gg)),
            pl.BlockSpec((L, D_STATE), lambda bb, gg, s: (r(bb, gg, s), c_blk + gg)),
            pl.BlockSpec((HEADS_PER_GROUP, L), lambda bb, gg, s: (dt_blk + gg, r(bb, gg, s))),
        ]

    pair_row = lambda bb, gg, s: bb * (nc // 2) + s
    chunk_inputs = (proj, proj, proj, dt_t)
    return pl.pallas_call(
        _ssd_kernel,
        grid=(batch, g, nc // 2),
        in_specs=[
            *chunk_specs(lambda s: 0),
            *chunk_specs(lambda s: 2 * s + 1),
            *chunk_specs(lambda s: jnp.minimum(2 * s + 2, nc - 1)),
            pl.BlockSpec((2 * L, GROUP_WIDTH), lambda bb, gg, s: (pair_row(bb, gg, s), z_blk + gg)),
            pl.BlockSpec((1,) + gp.shape[1:], lambda bb, gg, s: (gg, 0, 0)),
            pl.BlockSpec((1,) + hp.shape[1:], lambda bb, gg, s: (gg, 0, 0)),
            pl.BlockSpec((128, 2 * GROUP_WIDTH), lambda bb, gg, s: (0, 0)),
            pl.BlockSpec((L, L), lambda bb, gg, s: (0, 0)),
            pl.BlockSpec(((CONV_WIDTH - 1) * L, L), lambda bb, gg, s: (0, 0)),
        ],
        out_specs=pl.BlockSpec((2 * L, GROUP_WIDTH), lambda bb, gg, s: (pair_row(bb, gg, s), gg)),
        out_shape=jax.ShapeDtypeStruct((t, d_inner), BF16),
        scratch_shapes=[
            pltpu.VMEM((D_STATE, GROUP_WIDTH), F32),
            pltpu.VMEM((CONV_TAIL, GROUP_WIDTH + 2 * D_STATE), F32),
            *[pltpu.VMEM(shape, dtype) for _ in range(2) for _, shape, dtype in _SSD_SLOT],
        ],
        compiler_params=_params("parallel", "parallel", "arbitrary"),
        name="ssd",
    )(*chunk_inputs, *chunk_inputs, *chunk_inputs, proj, gp, hp, expand, tri, shift)


def _mix_kernel(attn_ref, ssm_ref, ga_ref, gs_ref, wa_ref, ws_ref, o_ref):
    a = _dot(attn_ref[...], wa_ref[...])
    s = _dot(ssm_ref[...], ws_ref[...])
    mixed = _sigmoid(ga_ref[...].astype(F32)) * a + _sigmoid(gs_ref[...].astype(F32)) * s
    o_ref[...] = mixed.astype(BF16)


def _mix(attn, ssm, proj, wa, ws, ga_blk, gs_blk, tm, tn):
    t = attn.shape[0]
    d = wa.shape[1]
    return pl.pallas_call(
        _mix_kernel,
        grid=(t // tm, d // tn),
        in_specs=[
            pl.BlockSpec((tm, attn.shape[1]), lambda m, n: (m, 0)),
            pl.BlockSpec((tm, ssm.shape[1]), lambda m, n: (m, 0)),
            pl.BlockSpec((tm, tn), lambda m, n: (m, ga_blk + n)),
            pl.BlockSpec((tm, tn), lambda m, n: (m, gs_blk + n)),
            pl.BlockSpec((wa.shape[0], tn), lambda m, n: (0, n)),
            pl.BlockSpec((ws.shape[0], tn), lambda m, n: (0, n)),
        ],
        out_specs=pl.BlockSpec((tm, tn), lambda m, n: (m, n)),
        out_shape=jax.ShapeDtypeStruct((t, d), BF16),
        compiler_params=_params("parallel", "arbitrary"),
        name="mix",
    )(attn, ssm, proj, proj, wa, ws)


def _mix_out_kernel(mixed_ref, x_ref, w_ref, post_ref, pre_ref, h_ref, f_ref):
    groups = _row_groups(x_ref.shape[0], ROW_GROUPS)
    outs = [_dot(mixed_ref[r, :], w_ref[...]) for r in groups]
    for r, o in zip(groups, outs):
        h = x_ref[r, :] + _rms(o, post_ref[...])
        h_ref[r, :] = h
        f_ref[r, :] = _rms(h, pre_ref[...]).astype(BF16)


def _mix_out(mixed, x2, w_out, post_w, pre_w, tm):
    t, d = x2.shape
    row = pl.BlockSpec((tm, d), lambda m: (m, 0))
    vec = pl.BlockSpec((1, d), lambda m: (0, 0))
    return pl.pallas_call(
        _mix_out_kernel,
        grid=(t // tm,),
        in_specs=[row, row, pl.BlockSpec((d, d), lambda m: (0, 0)), vec, vec],
        out_specs=[row, row],
        out_shape=[jax.ShapeDtypeStruct((t, d), F32), jax.ShapeDtypeStruct((t, d), BF16)],
        compiler_params=_params("parallel"),
        name="mix_out",
    )(mixed, x2, w_out, post_w, pre_w)


def _ffn_kernel(f_ref, wg_ref, wu_ref, wd_ref, o_ref, acc_scr):
    kf = pl.program_id(1)

    @pl.when(kf == 0)
    def _():
        acc_scr[...] = jnp.zeros(acc_scr.shape, F32)

    f = f_ref[...]
    g = _dot(f, wg_ref[...].astype(BF16))
    u = _dot(f, wu_ref[...].astype(BF16))
    a = (_silu(g) * u).astype(BF16)
    acc_scr[...] += _dot(a, wd_ref[...].astype(BF16))

    @pl.when(kf == pl.num_programs(1) - 1)
    def _():
        o_ref[...] = acc_scr[...].astype(BF16)


def _ffn(f, wg, wu, wd, tm, tf):
    t, d = f.shape
    dff = wg.shape[1]
    return pl.pallas_call(
        _ffn_kernel,
        grid=(t // tm, dff // tf),
        in_specs=[
            pl.BlockSpec((tm, d), lambda m, k: (m, 0)),
            pl.BlockSpec((d, tf), lambda m, k: (0, k)),
            pl.BlockSpec((d, tf), lambda m, k: (0, k)),
            pl.BlockSpec((tf, d), lambda m, k: (k, 0)),
        ],
        out_specs=pl.BlockSpec((tm, d), lambda m, k: (m, 0)),
        out_shape=jax.ShapeDtypeStruct((t, d), BF16),
        scratch_shapes=[pltpu.VMEM((tm, d), F32)],
        compiler_params=_params("parallel", "arbitrary"),
        name="ffn",
    )(f, wg, wu, wd)


def _ple_kernel(h_ref, d_ref, p_ref, fpost_ref, ppre_ref, ppost_ref, wg_ref, wp_ref, o_ref):
    groups = _row_groups(h_ref.shape[0], ROW_GROUPS)
    hs = [h_ref[r, :] + _rms(d_ref[r, :].astype(F32), fpost_ref[...]) for r in groups]
    gates = [_dot(_rms(h, ppre_ref[...]).astype(BF16), wg_ref[...]) for h in hs]
    es = [_dot(p_ref[r, :].astype(BF16), wp_ref[...]) * _sigmoid(g) for r, g in zip(groups, gates)]
    for r, h, e in zip(groups, hs, es):
        o_ref[r, :] = h + _rms(e, ppost_ref[...])


def _ple(h1, dff, p2, fpost, ppre, ppost, wg, wp, tm):
    t, d = h1.shape
    row = pl.BlockSpec((tm, d), lambda m: (m, 0))
    vec = pl.BlockSpec((1, d), lambda m: (0, 0))
    return pl.pallas_call(
        _ple_kernel,
        grid=(t // tm,),
        in_specs=[row, row, pl.BlockSpec((tm, p2.shape[1]), lambda m: (m, 0)), vec, vec, vec,
                  pl.BlockSpec((d, d), lambda m: (0, 0)),
                  pl.BlockSpec((p2.shape[1], d), lambda m: (0, 0))],
        out_specs=row,
        out_shape=jax.ShapeDtypeStruct((t, d), F32),
        compiler_params=_params("parallel"),
        name="ple",
    )(h1, dff, p2, fpost, ppre, ppost, wg, wp)


def _expand_matrix():
    e = np.zeros((128, 2 * GROUP_WIDTH), np.float32)
    for part in range(3):
        for r in range(HEADS_PER_GROUP):
            lo = r * SSM_HEADDIM
            e[part * HEADS_PER_GROUP + r, lo:lo + SSM_HEADDIM] = 1.0
            e[(3 + part) * HEADS_PER_GROUP + r, GROUP_WIDTH + lo:GROUP_WIDTH + lo + SSM_HEADDIM] = 1.0
    return e


def _layer(h, p_i, pos, invf, mix_norm_pre, mix_norm_post, w_in, q_norm, w_uq, kv_norm, w_ukv,
           conv_w, conv_b, dt_bias, a_log, d_skip, ssm_norm, w_attn_o, w_ssm_o, w_out,
           ffn_norm_pre, ffn_norm_post, w_gate, w_up, w_down,
           ple_norm_pre, ple_norm_post, w_ple_gate, w_ple):
    batch, seq, d = h.shape
    t = batch * seq
    d_inner = SSM_GROUPS * GROUP_WIDTH
    bc_width = SSM_GROUPS * D_STATE
    n_ssm_heads = SSM_GROUPS * HEADS_PER_GROUP
    vec = lambda v: v.reshape(1, -1).astype(F32)

    o_kr = Q_LORA + KV_LORA
    o_z = o_kr + QK_ROPE
    o_xbc = o_z + d_inner
    o_dt = o_xbc + d_inner + 2 * bc_width
    o_ga = o_dt + n_ssm_heads
    w_t = w_in.T.astype(BF16)
    segments = ((0, o_kr), (o_z, o_dt - o_z), (o_ga, w_in.shape[1] - o_ga))
    half = QK_ROPE // 2
    w_small_t = jnp.concatenate([w_t[o_kr:o_z], w_t[o_kr + half:o_z], w_t[o_kr:o_kr + half], w_t[o_dt:o_ga]], axis=0)
    dt_blk = 2 * QK_ROPE // HEADS_PER_GROUP
    z_blk = (Q_LORA + KV_LORA) // GROUP_WIDTH
    x_blk = z_blk + SSM_GROUPS
    b_blk = (Q_LORA + KV_LORA + 2 * d_inner) // D_STATE
    c_blk = b_blk + SSM_GROUPS
    g_off = Q_LORA + KV_LORA + 2 * d_inner + 2 * bc_width

    x2 = h.reshape(t, d)
    proj, small_t = _inproj(x2, vec(mix_norm_pre), w_t, segments, w_small_t, tm=1024, tn=1024)

    wq = w_uq.reshape(Q_LORA, N_HEADS_MLA, QK_DIM)
    wq = jnp.concatenate([wq, wq[:, :, QK_NOPE + half:], wq[:, :, QK_NOPE:QK_NOPE + half]], axis=2)
    wq_t = jnp.transpose(wq, (1, 2, 0)).reshape(N_HEADS_MLA * QK_PAD, Q_LORA).astype(BF16)
    wkv = w_ukv.reshape(KV_LORA, N_HEADS_MLA, QK_NOPE + V_DIM)
    wk = wkv[:, :, :QK_NOPE].reshape(KV_LORA, N_HEADS_MLA * QK_NOPE).astype(BF16)
    wv_t = jnp.transpose(wkv[:, :, QK_NOPE:], (1, 2, 0)).reshape(N_HEADS_MLA * V_DIM, KV_LORA).astype(BF16)
    qt, k, vt = _mla_prep(proj, small_t, pos, invf, vec(q_norm), vec(kv_norm), wq_t, wk, wv_t,
                          batch, seq, tm=512, hb=N_HEADS_MLA)
    attn = _attention(qt, k, vt, tq=1024).reshape(t, N_HEADS_MLA * V_DIM)

    expand = jnp.asarray(_expand_matrix(), BF16)
    tri = jnp.asarray(np.triu(np.ones((CHUNK, CHUNK), np.float32)), BF16)
    gp, hp = _ssd_group_params(conv_w, conv_b, d_skip, ssm_norm, dt_bias, a_log)
    ssm = _ssd(proj, small_t, gp, hp, expand, tri, jnp.asarray(_shift_matrices(), BF16), batch, seq,
               (z_blk, x_blk, b_blk, c_blk, dt_blk))

    tn_mix = 512
    mixed = _mix(attn, ssm, proj, w_attn_o.astype(BF16), w_ssm_o.astype(BF16),
                 g_off // tn_mix, (g_off + d) // tn_mix, tm=1024, tn=tn_mix)
    h1, f = _mix_out(mixed, x2, w_out.astype(BF16), vec(mix_norm_post), vec(ffn_norm_pre), tm=512)
    dff = _ffn(f, w_gate, w_up, w_down, tm=1024, tf=512)
    out = _ple(h1, dff, p_i.reshape(t, -1), vec(ffn_norm_post), vec(ple_norm_pre), vec(ple_norm_post),
               w_ple_gate.astype(BF16), w_ple.astype(BF16), tm=512)
    return out.reshape(batch, seq, d)


def kernel(x, p, positions, mix_norm_pre, mix_norm_post, w_in, q_norm, w_uq, kv_norm, w_ukv, conv_w, conv_b, dt_bias, a_log, d_skip, ssm_norm, w_attn_o, w_ssm_o, w_out, ffn_norm_pre, ffn_norm_post, w_gate, w_up, w_down, ple_norm_pre, ple_norm_post, w_ple_gate, w_ple):
    invf = ROPE_THETA ** (-jnp.arange(0, QK_ROPE, 2, dtype=F32) / QK_ROPE)
    invf = invf.reshape(QK_ROPE // 2, 1)
    pos = positions.reshape(1, -1)
    h = x
    for i in range(w_in.shape[0]):
        h = _layer(h, p[i], pos, invf, mix_norm_pre[i], mix_norm_post[i], w_in[i], q_norm[i], w_uq[i],
                   kv_norm[i], w_ukv[i], conv_w[i], conv_b[i], dt_bias[i], a_log[i], d_skip[i], ssm_norm[i],
                   w_attn_o[i], w_ssm_o[i], w_out[i], ffn_norm_pre[i], ffn_norm_post[i],
                   w_gate[i], w_up[i], w_down[i], ple_norm_pre[i], ple_norm_post[i], w_ple_gate[i], w_ple[i])
    return h
```

```python
import functools
import math

import jax
import jax.numpy as jnp
import numpy as np
from jax import lax
from jax.experimental import pallas as pl
from jax.experimental.pallas import tpu as pltpu

F32 = jnp.float32
BF16 = jnp.bfloat16

EPS = 1e-6
ROPE_THETA = 10000.0
LOG2_E = 1.4426950408889634

N_HEADS_MLA = 16
Q_LORA = 512
KV_LORA = 512
QK_NOPE = 128
QK_ROPE = 64
V_DIM = 128
QK_DIM = QK_NOPE + QK_ROPE
QK_PAD = 256
VT_ROWS = V_DIM + 16

SSM_HEADDIM = 64
SSM_GROUPS = 8
HEADS_PER_GROUP = 8
GROUP_WIDTH = HEADS_PER_GROUP * SSM_HEADDIM
D_STATE = 128
CONV_WIDTH = 4
CHUNK = 256
CONV_TAIL = 8

V7X_VMEM_BYTES = 64 * 1024 * 1024
VMEM_LIMIT = V7X_VMEM_BYTES * 7 // 8

TILE_INPROJ = (1024, 1024)
TILE_PREP_ROWS = 512
TILE_ATTN = 1024
TILE_MIX = (1024, 512)
TILE_MIX_OUT_ROWS = 512
TILE_FFN = (1024, 512)
TILE_PLE_ROWS = 512


def _params(*sem):
    return pltpu.CompilerParams(dimension_semantics=sem, vmem_limit_bytes=VMEM_LIMIT)


def _rms(x, w):
    return x * lax.rsqrt(jnp.mean(x * x, axis=-1, keepdims=True) + EPS) * w


def _sigmoid(x):
    return 0.5 + 0.5 * jnp.tanh(0.5 * x)


def _silu(x):
    h = 0.5 * x
    return h + h * jnp.tanh(h)


def _dot(a, b):
    return jnp.dot(a, b, preferred_element_type=F32)


ROW_GROUPS = 4


def _row_groups(rows, n):
    step = rows // n
    return [slice(i * step, (i + 1) * step) for i in range(n)]


def _dot_nt(a, b):
    return lax.dot_general(a, b, (((1,), (1,)), ((), ())), preferred_element_type=F32)


def _inproj_kernel(x_ref, nw_ref, w_ref, wt_ref, o_ref, ot_ref, u_scr):
    @pl.when(pl.program_id(1) == 0)
    def _():
        u = _rms(x_ref[...], nw_ref[...]).astype(BF16)
        u_scr[...] = u
        ot_ref[...] = _dot_nt(wt_ref[...], u)

    o_ref[...] = _dot_nt(u_scr[...], w_ref[...]).astype(BF16)


def _inproj(x2, norm_w, w_t, segments, w_small_t, tm, tn):
    t, d = x2.shape
    bounds = [0]
    for _, count in segments:
        bounds.append(bounds[-1] + count // tn)
    n_main = bounds[-1] * tn
    n_small = w_small_t.shape[0]

    def w_row(n):
        row = segments[0][0] + n * tn
        for (start, _), lo in zip(segments[1:], bounds[1:]):
            row = jnp.where(n >= lo, start + (n - lo) * tn, row)
        return pl.multiple_of(row, math.gcd(tn, *[start for start, _ in segments]))

    return pl.pallas_call(
        _inproj_kernel,
        grid=(t // tm, n_main // tn),
        in_specs=[
            pl.BlockSpec((tm, d), lambda m, n: (m, 0)),
            pl.BlockSpec((1, d), lambda m, n: (0, 0)),
            pl.BlockSpec((pl.Element(tn), pl.Element(d)), lambda m, n: (w_row(n), 0)),
            pl.BlockSpec((n_small, d), lambda m, n: (0, 0)),
        ],
        out_specs=[
            pl.BlockSpec((tm, tn), lambda m, n: (m, n)),
            pl.BlockSpec((n_small, tm), lambda m, n: (0, m)),
        ],
        out_shape=[
            jax.ShapeDtypeStruct((t, n_main), BF16),
            jax.ShapeDtypeStruct((n_small, t), F32),
        ],
        scratch_shapes=[pltpu.VMEM((tm, d), BF16)],
        compiler_params=_params("parallel", "arbitrary"),
        name="inproj",
    )(x2, norm_w, w_t, w_small_t)


def _mla_prep_kernel(cq_ref, ckv_ref, kr_ref, pos_ref, invf_ref, qn_ref, kvn_ref, wq_ref, wk_ref, wv_ref,
                     qt_ref, k_ref, vt_ref, cqn_scr, ckvn_scr, tab_scr, kpe_scr):
    def rope_t(pair):
        t = pair * tab_scr[...]
        return t[:QK_ROPE] + t[QK_ROPE:]

    @pl.when(pl.program_id(1) == 0)
    def _():
        cqn_scr[...] = _rms(cq_ref[...].astype(F32), qn_ref[...]).astype(BF16)
        ckvn_scr[...] = _rms(ckv_ref[...].astype(F32), kvn_ref[...]).astype(BF16)
        ang = invf_ref[...] * pos_ref[...].astype(F32)
        cos, sin = jnp.cos(ang), jnp.sin(ang)
        tab_scr[...] = jnp.concatenate([cos, cos, -sin, sin], axis=0)
        kpe_t = rope_t(kr_ref[...])
        kpe_scr[...] = jnp.concatenate([kpe_t, jnp.zeros_like(kpe_t)], axis=0).T.astype(BF16)

    scale = QK_DIM ** -0.5 * LOG2_E
    ckvn = ckvn_scr[...]
    qo = _dot_nt(wq_ref[...], cqn_scr[...])
    kn = _dot(ckvn, wk_ref[...]).astype(BF16)
    vo = _dot_nt(wv_ref[...], ckvn).astype(BF16)
    pad_row = lax.broadcasted_iota(jnp.int32, (VT_ROWS - V_DIM, vo.shape[1]), 0)
    ones_rows = jnp.where(pad_row == 0, 1.0, 0.0).astype(BF16)
    for hh in range(qt_ref.shape[1]):
        qh = qo[hh * QK_PAD:(hh + 1) * QK_PAD]
        qt_ref[0, hh, :QK_NOPE, :] = (qh[:QK_NOPE] * scale).astype(BF16)
        q_pe = (rope_t(qh[QK_NOPE:]) * scale).astype(BF16)
        qt_ref[0, hh, QK_NOPE:QK_DIM, :] = q_pe
        qt_ref[0, hh, QK_DIM:, :] = jnp.zeros_like(q_pe)
        k_ref[0, hh, :, :QK_NOPE] = kn[:, hh * QK_NOPE:(hh + 1) * QK_NOPE]
        k_ref[0, hh, :, QK_NOPE:] = kpe_scr[...]
        vt_ref[0, hh, :V_DIM, :] = vo[hh * V_DIM:(hh + 1) * V_DIM]
        vt_ref[0, hh, V_DIM:, :] = ones_rows


def _mla_prep(proj, small_t, pos, invf, q_norm, kv_norm, wq_t, wk, wv_t, batch, seq, tm, hb):
    t = proj.shape[0]
    spb = seq // tm
    h = N_HEADS_MLA
    return pl.pallas_call(
        _mla_prep_kernel,
        grid=(t // tm, h // hb),
        in_specs=[
            pl.BlockSpec((tm, Q_LORA), lambda m, hh: (m, 0)),
            pl.BlockSpec((tm, KV_LORA), lambda m, hh: (m, 1)),
            pl.BlockSpec((2 * QK_ROPE, tm), lambda m, hh: (0, m)),
            pl.BlockSpec((1, tm), lambda m, hh: (0, m)),
            pl.BlockSpec((QK_ROPE // 2, 1), lambda m, hh: (0, 0)),
            pl.BlockSpec((1, Q_LORA), lambda m, hh: (0, 0)),
            pl.BlockSpec((1, KV_LORA), lambda m, hh: (0, 0)),
            pl.BlockSpec((hb * QK_PAD, Q_LORA), lambda m, hh: (hh, 0)),
            pl.BlockSpec((KV_LORA, hb * QK_NOPE), lambda m, hh: (0, hh)),
            pl.BlockSpec((hb * V_DIM, KV_LORA), lambda m, hh: (hh, 0)),
        ],
        out_specs=[
            pl.BlockSpec((1, hb, QK_PAD, tm), lambda m, hh: (m // spb, hh, 0, m % spb)),
            pl.BlockSpec((1, hb, tm, QK_PAD), lambda m, hh: (m // spb, hh, m % spb, 0)),
            pl.BlockSpec((1, hb, VT_ROWS, tm), lambda m, hh: (m // spb, hh, 0, m % spb)),
        ],
        out_shape=[
            jax.ShapeDtypeStruct((batch, h, QK_PAD, seq), BF16),
            jax.ShapeDtypeStruct((batch, h, seq, QK_PAD), BF16),
            jax.ShapeDtypeStruct((batch, h, VT_ROWS, seq), BF16),
        ],
        scratch_shapes=[
            pltpu.VMEM((tm, Q_LORA), BF16),
            pltpu.VMEM((tm, KV_LORA), BF16),
            pltpu.VMEM((2 * QK_ROPE, tm), F32),
            pltpu.VMEM((tm, 2 * QK_ROPE), BF16),
        ],
        compiler_params=_params("parallel", "arbitrary"),
        name="mla_prep",
    )(proj, proj, small_t, pos, invf, q_norm, kv_norm, wq_t, wk, wv_t)


def _attn_kernel(qt_ref, k_ref, vt_ref, o_ref, m_scr, acc_scr, sa_scr, sb_scr, *, tq):
    qi = pl.program_id(2)
    m_scr[...] = jnp.full(m_scr.shape, -jnp.inf, F32)
    acc_scr[...] = jnp.zeros(acc_scr.shape, F32)

    def scores(ki):
        start = pl.multiple_of(ki * tq, tq)
        return _dot(k_ref[0, 0, pl.ds(start, tq), :], qt_ref[0, 0])

    def accumulate(ki, s):
        start = pl.multiple_of(ki * tq, tq)
        m_prev = m_scr[...]
        m_new = jnp.maximum(m_prev, jnp.max(s, axis=0, keepdims=True))
        p = jnp.exp2(s - m_new).astype(BF16)
        acc_scr[...] = jnp.exp2(m_prev - m_new) * acc_scr[...] + _dot(vt_ref[0, 0, :, pl.ds(start, tq)], p)
        m_scr[...] = m_new

    def accumulate_diagonal(s):
        key = lax.broadcasted_iota(jnp.int32, s.shape, 0)
        qry = lax.broadcasted_iota(jnp.int32, s.shape, 1)
        accumulate(qi, jnp.where(key <= qry, s, -jnp.inf))

    sa_scr[...] = scores(0)

    def pair(j, carry):
        sb_scr[...] = scores(2 * j + 1)
        accumulate(2 * j, sa_scr[...])
        sa_scr[...] = scores(2 * j + 2)
        accumulate(2 * j + 1, sb_scr[...])
        return carry

    lax.fori_loop(0, qi // 2, pair, 0)

    @pl.when(qi % 2 == 0)
    def _():
        accumulate_diagonal(sa_scr[...])

    @pl.when(qi % 2 == 1)
    def _():
        sb_scr[...] = scores(qi)
        accumulate(qi - 1, sa_scr[...])
        accumulate_diagonal(sb_scr[...])

    acc = acc_scr[...]
    o_ref[0] = (acc[:V_DIM] / acc[V_DIM:V_DIM + 1]).T.astype(BF16)


def _attention(qt, k, vt, tq):
    b, h, s, _ = k.shape
    return pl.pallas_call(
        functools.partial(_attn_kernel, tq=tq),
        grid=(b, h, s // tq),
        in_specs=[
            pl.BlockSpec((1, 1, QK_PAD, tq), lambda bb, hh, qi: (bb, hh, 0, qi)),
            pl.BlockSpec((1, 1, s, QK_PAD), lambda bb, hh, qi: (bb, hh, 0, 0)),
            pl.BlockSpec((1, 1, VT_ROWS, s), lambda bb, hh, qi: (bb, hh, 0, 0)),
        ],
        out_specs=pl.BlockSpec((1, tq, V_DIM), lambda bb, hh, qi: (bb, qi, hh)),
        out_shape=jax.ShapeDtypeStruct((b, s, h * V_DIM), BF16),
        scratch_shapes=[
            pltpu.VMEM((1, tq), F32),
            pltpu.VMEM((VT_ROWS, tq), F32),
            pltpu.VMEM((tq, tq), F32),
            pltpu.VMEM((tq, tq), F32),
        ],
        compiler_params=_params("parallel", "parallel", "arbitrary"),
        name="attention",
    )(qt, k, vt)


def _split3(x):
    hi = x.astype(BF16).astype(F32)
    r = x - hi
    mid = r.astype(BF16).astype(F32)
    return hi, mid, r - mid


_SSD_SLOT = (("xdt", (CHUNK, GROUP_WIDTH), BF16),
             ("xw", (CHUNK, GROUP_WIDTH), BF16),
             ("ecum", (CHUNK, GROUP_WIDTH), F32),
             ("dsx", (CHUNK, GROUP_WIDTH), F32),
             ("dlast", (1, GROUP_WIDTH), F32),
             ("cc", (CHUNK, D_STATE), BF16),
             ("bc", (CHUNK, D_STATE), BF16),
             ("bct", (D_STATE, CHUNK), BF16),
             ("cum", (HEADS_PER_GROUP, CHUNK), F32),
             ("cols", (CHUNK, 128), F32))


def _ssd_kernel(*refs):
    n_fixed = 21
    (x0_ref, b0_ref, c0_ref, dt0_ref, x1_ref, b1_ref, c1_ref, dt1_ref, x2_ref, b2_ref, c2_ref, dt2_ref,
     z_ref, gp_ref, hp_ref, expand_ref, tri_ref, shift_ref, o_ref, state_scr, tail_scr) = refs[:n_fixed]
    names = [n for n, _, _ in _SSD_SLOT]
    slots = [dict(zip(names, refs[n_fixed + i * len(names):n_fixed + (i + 1) * len(names)])) for i in range(2)]
    L = CHUNK
    gp = gp_ref[0]
    cw = gp[:CONV_WIDTH]
    bias = gp[CONV_WIDTH:CONV_WIDTH + 1]
    dskip = gp[CONV_WIDTH + 1:CONV_WIDTH + 2, :GROUP_WIDTH]
    norm_w = gp[CONV_WIDTH + 2:CONV_WIDTH + 3, :GROUP_WIDTH]
    hp = hp_ref[0]

    def stage_a(x_ref, b_ref, c_ref, dt_ref, out):
        raw = jnp.concatenate([x_ref[...], b_ref[...], c_ref[...]], axis=1)
        shifted = _dot(shift_ref[...], raw)
        rawf = raw.astype(F32)
        conv = bias + cw[CONV_WIDTH - 1:CONV_WIDTH, :] * rawf
        for k in range(1, CONV_WIDTH):
            conv = conv + cw[CONV_WIDTH - 1 - k:CONV_WIDTH - k, :] * shifted[(k - 1) * L:k * L]
        ext = jnp.concatenate([tail_scr[...], rawf[:CONV_TAIL]], axis=0)
        head = bias
        for tap in range(CONV_WIDTH):
            off = CONV_TAIL - (CONV_WIDTH - 1) + tap
            head = head + cw[tap:tap + 1, :] * ext[off:off + CONV_TAIL]
        conv = jnp.concatenate([head, conv[CONV_TAIL:]], axis=0)
        tail_scr[...] = rawf[L - CONV_TAIL:, :]
        act = _silu(conv)
        xc = act[:, :GROUP_WIDTH]
        bc = act[:, GROUP_WIDTH:GROUP_WIDTH + D_STATE]
        cc = act[:, GROUP_WIDTH + D_STATE:]

        dtr = dt_ref[...] + hp[:, 0:1]
        dt = jnp.maximum(dtr, 0.0) + jnp.log1p(jnp.exp(-jnp.abs(dtr)))
        da = dt * (-jnp.exp(hp[:, 1:2])) * LOG2_E
        d_hi, d_mid, d_lo = _split3(da)
        tri = tri_ref[...]
        cum = _dot(d_hi.astype(BF16), tri) + _dot(d_mid.astype(BF16), tri) + _dot(d_lo.astype(BF16), tri)

        t_hi, t_mid, t_lo = _split3(dt)
        c_hi, c_mid, c_lo = _split3(cum)
        stacked = jnp.concatenate(
            [t_hi, t_mid, t_lo, c_hi, c_mid, c_lo, cum, jnp.zeros((128 - 7 * HEADS_PER_GROUP, L), F32)], axis=0)
        cols = stacked.T
        expanded = _dot(cols.astype(BF16), expand_ref[...])
        dt_e = expanded[:, :GROUP_WIDTH]
        cum_e = expanded[:, GROUP_WIDTH:]
        last_e = cum_e[L - 1:L, :]
        xdt = xc * dt_e
        out["xdt"][...] = xdt.astype(BF16)
        out["xw"][...] = (xdt * jnp.exp2(last_e - cum_e)).astype(BF16)
        out["ecum"][...] = jnp.exp2(cum_e)
        out["dsx"][...] = dskip * xc
        out["dlast"][...] = jnp.exp2(last_e)
        out["cc"][...] = cc.astype(BF16)
        out["bc"][...] = bc.astype(BF16)
        out["bct"][...] = bc.T.astype(BF16)
        out["cum"][...] = cum
        out["cols"][...] = cols

    def stage_b(inp, rows):
        state = state_scr[...]
        cc_b = inp["cc"][...]
        y = _dot(cc_b, state.astype(BF16)) * inp["ecum"][...] + inp["dsx"][...]
        state_scr[...] = state * inp["dlast"][...] + _dot(inp["bct"][...], inp["xw"][...])

        cb = _dot_nt(cc_b, inp["bc"][...])
        ri = lax.broadcasted_iota(jnp.int32, (L, L), 0)
        ci = lax.broadcasted_iota(jnp.int32, (L, L), 1)
        causal = ci <= ri
        lane = lax.broadcasted_iota(jnp.int32, (L, 128), 1)
        cum = inp["cum"][...]
        cols = inp["cols"][...]
        pairs = []
        for pr in range(HEADS_PER_GROUP // 2):
            xp = inp["xdt"][:, pr * 128:(pr + 1) * 128]
            acc = None
            for sub in range(2):
                r = 2 * pr + sub
                col = cols[:, 6 * HEADS_PER_GROUP + r:6 * HEADS_PER_GROUP + r + 1]
                seg = col - cum[r:r + 1, :]
                m = (cb * jnp.exp2(jnp.where(causal, seg, -jnp.inf))).astype(BF16)
                keep = (lane < SSM_HEADDIM) if sub == 0 else (lane >= SSM_HEADDIM)
                part = _dot(m, jnp.where(keep, xp, jnp.zeros_like(xp)))
                acc = part if acc is None else acc + part
            pairs.append(acc)
        y = y + jnp.concatenate(pairs, axis=1)
        y = y * _silu(z_ref[rows, :].astype(F32))
        o_ref[rows, :] = _rms(y, norm_w).astype(BF16)

    @pl.when(pl.program_id(2) == 0)
    def _():
        state_scr[...] = jnp.zeros(state_scr.shape, F32)
        tail_scr[...] = jnp.zeros(tail_scr.shape, F32)
        stage_a(x0_ref, b0_ref, c0_ref, dt0_ref, slots[0])

    stage_a(x1_ref, b1_ref, c1_ref, dt1_ref, slots[1])
    stage_b(slots[0], slice(0, L))
    stage_a(x2_ref, b2_ref, c2_ref, dt2_ref, slots[0])
    stage_b(slots[1], slice(L, 2 * L))


def _shift_matrices():
    L = CHUNK
    shift = np.zeros(((CONV_WIDTH - 1) * L, L), np.float32)
    for k in range(1, CONV_WIDTH):
        for t in range(k, L):
            shift[(k - 1) * L + t, t - k] = 1.0
    return shift


def _ssd_group_params(conv_w, conv_b, d_skip, ssm_norm, dt_bias, a_log):
    g, gw, n = SSM_GROUPS, GROUP_WIDTH, D_STATE
    d_inner = g * gw
    per_group = lambda v: jnp.concatenate(
        [v[:, :d_inner].reshape(-1, g, gw), v[:, d_inner:d_inner + g * n].reshape(-1, g, n),
         v[:, d_inner + g * n:].reshape(-1, g, n)], axis=2).transpose(1, 0, 2)
    pad = lambda v: jnp.pad(v.reshape(g, 1, gw), ((0, 0), (0, 0), (0, 2 * n)))
    gp = jnp.concatenate([per_group(conv_w.astype(F32)), per_group(conv_b.astype(F32).reshape(1, -1)),
                          pad(jnp.repeat(d_skip.astype(F32), SSM_HEADDIM)), pad(ssm_norm.astype(F32)),
                          jnp.zeros((g, 8 - CONV_WIDTH - 3, gw + 2 * n), F32)], axis=1)
    hp = jnp.stack([dt_bias.astype(F32).reshape(g, HEADS_PER_GROUP), a_log.astype(F32).reshape(g, HEADS_PER_GROUP)],
                   axis=2)
    hp = jnp.pad(hp, ((0, 0), (0, 0), (0, 128 - hp.shape[2])))
    return gp, hp


def _ssd(proj, dt_t, gp, hp, expand, tri, shift, batch, seq, offs):
    t = proj.shape[0]
    L = CHUNK
    nc = seq // L
    g = SSM_GROUPS
    d_inner = g * GROUP_WIDTH
    row = lambda bb, gg, c: bb * nc + c
    z_blk, x_blk, b_blk, c_blk, dt_blk = offs
    def chunk_specs(chunk_of_step):
        r = lambda bb, gg, s: row(bb, gg, chunk_of_step(s))
        return [
            pl.BlockSpec((L, GROUP_WIDTH), lambda bb, gg, s: (r(bb, gg, s), x_blk + gg)),
            pl.BlockSpec((L, D_STATE), lambda bb, gg, s: (r(bb, gg, s), b_blk + gg)),
            pl.BlockSpec((L, D_STATE), lambda bb, gg, s: (r(bb, gg, s), c_blk + gg)),
            pl.BlockSpec((HEADS_PER_GROUP, L), lambda bb, gg, s: (dt_blk + gg, r(bb, gg, s))),
        ]

    pair_row = lambda bb, gg, s: bb * (nc // 2) + s
    chunk_inputs = (proj, proj, proj, dt_t)
    return pl.pallas_call(
        _ssd_kernel,
        grid=(batch, g, nc // 2),
        in_specs=[
            *chunk_specs(lambda s: 0),
            *chunk_specs(lambda s: 2 * s + 1),
            *chunk_specs(lambda s: jnp.minimum(2 * s + 2, nc - 1)),
            pl.BlockSpec((2 * L, GROUP_WIDTH), lambda bb, gg, s: (pair_row(bb, gg, s), z_blk + gg)),
            pl.BlockSpec((1,) + gp.shape[1:], lambda bb, gg, s: (gg, 0, 0)),
            pl.BlockSpec((1,) + hp.shape[1:], lambda bb, gg, s: (gg, 0, 0)),
            pl.BlockSpec((128, 2 * GROUP_WIDTH), lambda bb, gg, s: (0, 0)),
            pl.BlockSpec((L, L), lambda bb, gg, s: (0, 0)),
            pl.BlockSpec(((CONV_WIDTH - 1) * L, L), lambda bb, gg, s: (0, 0)),
        ],
        out_specs=pl.BlockSpec((2 * L, GROUP_WIDTH), lambda bb, gg, s: (pair_row(bb, gg, s), gg)),
        out_shape=jax.ShapeDtypeStruct((t, d_inner), BF16),
        scratch_shapes=[
            pltpu.VMEM((D_STATE, GROUP_WIDTH), F32),
            pltpu.VMEM((CONV_TAIL, GROUP_WIDTH + 2 * D_STATE), F32),
            *[pltpu.VMEM(shape, dtype) for _ in range(2) for _, shape, dtype in _SSD_SLOT],
        ],
        compiler_params=_params("parallel", "parallel", "arbitrary"),
        name="ssd",
    )(*chunk_inputs, *chunk_inputs, *chunk_inputs, proj, gp, hp, expand, tri, shift)


def _mix_kernel(attn_ref, ssm_ref, ga_ref, gs_ref, wa_ref, ws_ref, o_ref):
    a = _dot(attn_ref[...], wa_ref[...])
    s = _dot(ssm_ref[...], ws_ref[...])
    mixed = _sigmoid(ga_ref[...].astype(F32)) * a + _sigmoid(gs_ref[...].astype(F32)) * s
    o_ref[...] = mixed.astype(BF16)


def _mix(attn, ssm, proj, wa, ws, ga_blk, gs_blk, tm, tn):
    t = attn.shape[0]
    d = wa.shape[1]
    return pl.pallas_call(
        _mix_kernel,
        grid=(t // tm, d // tn),
        in_specs=[
            pl.BlockSpec((tm, attn.shape[1]), lambda m, n: (m, 0)),
            pl.BlockSpec((tm, ssm.shape[1]), lambda m, n: (m, 0)),
            pl.BlockSpec((tm, tn), lambda m, n: (m, ga_blk + n)),
            pl.BlockSpec((tm, tn), lambda m, n: (m, gs_blk + n)),
            pl.BlockSpec((wa.shape[0], tn), lambda m, n: (0, n)),
            pl.BlockSpec((ws.shape[0], tn), lambda m, n: (0, n)),
        ],
        out_specs=pl.BlockSpec((tm, tn), lambda m, n: (m, n)),
        out_shape=jax.ShapeDtypeStruct((t, d), BF16),
        compiler_params=_params("parallel", "arbitrary"),
        name="mix",
    )(attn, ssm, proj, proj, wa, ws)


def _mix_out_kernel(mixed_ref, x_ref, w_ref, post_ref, pre_ref, h_ref, f_ref):
    groups = _row_groups(x_ref.shape[0], ROW_GROUPS)
    outs = [_dot(mixed_ref[r, :], w_ref[...]) for r in groups]
    for r, o in zip(groups, outs):
        h = x_ref[r, :] + _rms(o, post_ref[...])
        h_ref[r, :] = h
        f_ref[r, :] = _rms(h, pre_ref[...]).astype(BF16)


def _mix_out(mixed, x2, w_out, post_w, pre_w, tm):
    t, d = x2.shape
    row = pl.BlockSpec((tm, d), lambda m: (m, 0))
    vec = pl.BlockSpec((1, d), lambda m: (0, 0))
    return pl.pallas_call(
        _mix_out_kernel,
        grid=(t // tm,),
        in_specs=[row, row, pl.BlockSpec((d, d), lambda m: (0, 0)), vec, vec],
        out_specs=[row, row],
        out_shape=[jax.ShapeDtypeStruct((t, d), F32), jax.ShapeDtypeStruct((t, d), BF16)],
        compiler_params=_params("parallel"),
        name="mix_out",
    )(mixed, x2, w_out, post_w, pre_w)


def _ffn_kernel(f_ref, wg_ref, wu_ref, wd_ref, o_ref, acc_scr):
    kf = pl.program_id(1)

    @pl.when(kf == 0)
    def _():
        acc_scr[...] = jnp.zeros(acc_scr.shape, F32)

    f = f_ref[...]
    g = _dot(f, wg_ref[...].astype(BF16))
    u = _dot(f, wu_ref[...].astype(BF16))
    a = (_silu(g) * u).astype(BF16)
    acc_scr[...] += _dot(a, wd_ref[...].astype(BF16))

    @pl.when(kf == pl.num_programs(1) - 1)
    def _():
        o_ref[...] = acc_scr[...].astype(BF16)


def _ffn(f, wg, wu, wd, tm, tf):
    t, d = f.shape
    dff = wg.shape[1]
    return pl.pallas_call(
        _ffn_kernel,
        grid=(t // tm, dff // tf),
        in_specs=[
            pl.BlockSpec((tm, d), lambda m, k: (m, 0)),
            pl.BlockSpec((d, tf), lambda m, k: (0, k)),
            pl.BlockSpec((d, tf), lambda m, k: (0, k)),
            pl.BlockSpec((tf, d), lambda m, k: (k, 0)),
        ],
        out_specs=pl.BlockSpec((tm, d), lambda m, k: (m, 0)),
        out_shape=jax.ShapeDtypeStruct((t, d), BF16),
        scratch_shapes=[pltpu.VMEM((tm, d), F32)],
        compiler_params=_params("parallel", "arbitrary"),
        name="ffn",
    )(f, wg, wu, wd)


def _ple_kernel(h_ref, d_ref, p_ref, fpost_ref, ppre_ref, ppost_ref, wg_ref, wp_ref, o_ref):
    groups = _row_groups(h_ref.shape[0], ROW_GROUPS)
    hs = [h_ref[r, :] + _rms(d_ref[r, :].astype(F32), fpost_ref[...]) for r in groups]
    gates = [_dot(_rms(h, ppre_ref[...]).astype(BF16), wg_ref[...]) for h in hs]
    es = [_dot(p_ref[r, :].astype(BF16), wp_ref[...]) * _sigmoid(g) for r, g in zip(groups, gates)]
    for r, h, e in zip(groups, hs, es):
        o_ref[r, :] = h + _rms(e, ppost_ref[...])


def _ple(h1, dff, p2, fpost, ppre, ppost, wg, wp, tm):
    t, d = h1.shape
    row = pl.BlockSpec((tm, d), lambda m: (m, 0))
    vec = pl.BlockSpec((1, d), lambda m: (0, 0))
    return pl.pallas_call(
        _ple_kernel,
        grid=(t // tm,),
        in_specs=[row, row, pl.BlockSpec((tm, p2.shape[1]), lambda m: (m, 0)), vec, vec, vec,
                  pl.BlockSpec((d, d), lambda m: (0, 0)),
                  pl.BlockSpec((p2.shape[1], d), lambda m: (0, 0))],
        out_specs=row,
        out_shape=jax.ShapeDtypeStruct((t, d), F32),
        compiler_params=_params("parallel"),
        name="ple",
    )(h1, dff, p2, fpost, ppre, ppost, wg, wp)


def _expand_matrix():
    e = np.zeros((128, 2 * GROUP_WIDTH), np.float32)
    for part in range(3):
        for r in range(HEADS_PER_GROUP):
            lo = r * SSM_HEADDIM
            e[part * HEADS_PER_GROUP + r, lo:lo + SSM_HEADDIM] = 1.0
            e[(3 + part) * HEADS_PER_GROUP + r, GROUP_WIDTH + lo:GROUP_WIDTH + lo + SSM_HEADDIM] = 1.0
    return e


def _layer(h, p_i, pos, invf, mix_norm_pre, mix_norm_post, w_in, q_norm, w_uq, kv_norm, w_ukv,
           conv_w, conv_b, dt_bias, a_log, d_skip, ssm_norm, w_attn_o, w_ssm_o, w_out,
           ffn_norm_pre, ffn_norm_post, w_gate, w_up, w_down,
           ple_norm_pre, ple_norm_post, w_ple_gate, w_ple):
    batch, seq, d = h.shape
    t = batch * seq
    d_inner = SSM_GROUPS * GROUP_WIDTH
    bc_width = SSM_GROUPS * D_STATE
    n_ssm_heads = SSM_GROUPS * HEADS_PER_GROUP
    vec = lambda v: v.reshape(1, -1).astype(F32)

    o_kr = Q_LORA + KV_LORA
    o_z = o_kr + QK_ROPE
    o_xbc = o_z + d_inner
    o_dt = o_xbc + d_inner + 2 * bc_width
    o_ga = o_dt + n_ssm_heads
    w_t = w_in.T.astype(BF16)
    segments = ((0, o_kr), (o_z, o_dt - o_z), (o_ga, w_in.shape[1] - o_ga))
    half = QK_ROPE // 2
    w_small_t = jnp.concatenate([w_t[o_kr:o_z], w_t[o_kr + half:o_z], w_t[o_kr:o_kr + half], w_t[o_dt:o_ga]], axis=0)
    dt_blk = 2 * QK_ROPE // HEADS_PER_GROUP
    z_blk = (Q_LORA + KV_LORA) // GROUP_WIDTH
    x_blk = z_blk + SSM_GROUPS
    b_blk = (Q_LORA + KV_LORA + 2 * d_inner) // D_STATE
    c_blk = b_blk + SSM_GROUPS
    g_off = Q_LORA + KV_LORA + 2 * d_inner + 2 * bc_width

    row_tiles = (TILE_INPROJ[0], TILE_PREP_ROWS, TILE_MIX[0], TILE_MIX_OUT_ROWS, TILE_FFN[0], TILE_PLE_ROWS)
    assert seq % TILE_ATTN == 0 and seq % (2 * CHUNK) == 0 and seq % TILE_PREP_ROWS == 0, seq
    assert all(t % tile == 0 for tile in row_tiles), (t, row_tiles)
    assert d % TILE_MIX[1] == 0 and w_gate.shape[1] % TILE_FFN[1] == 0

    x2 = h.reshape(t, d)
    proj, small_t = _inproj(x2, vec(mix_norm_pre), w_t, segments, w_small_t,
                            tm=TILE_INPROJ[0], tn=TILE_INPROJ[1])

    wq = w_uq.reshape(Q_LORA, N_HEADS_MLA, QK_DIM)
    wq = jnp.concatenate([wq, wq[:, :, QK_NOPE + half:], wq[:, :, QK_NOPE:QK_NOPE + half]], axis=2)
    wq_t = jnp.transpose(wq, (1, 2, 0)).reshape(N_HEADS_MLA * QK_PAD, Q_LORA).astype(BF16)
    wkv = w_ukv.reshape(KV_LORA, N_HEADS_MLA, QK_NOPE + V_DIM)
    wk = wkv[:, :, :QK_NOPE].reshape(KV_LORA, N_HEADS_MLA * QK_NOPE).astype(BF16)
    wv_t = jnp.transpose(wkv[:, :, QK_NOPE:], (1, 2, 0)).reshape(N_HEADS_MLA * V_DIM, KV_LORA).astype(BF16)
    qt, k, vt = _mla_prep(proj, small_t, pos, invf, vec(q_norm), vec(kv_norm), wq_t, wk, wv_t,
                          batch, seq, tm=TILE_PREP_ROWS, hb=N_HEADS_MLA)
    attn = _attention(qt, k, vt, tq=TILE_ATTN).reshape(t, N_HEADS_MLA * V_DIM)

    expand = jnp.asarray(_expand_matrix(), BF16)
    tri = jnp.asarray(np.triu(np.ones((CHUNK, CHUNK), np.float32)), BF16)
    gp, hp = _ssd_group_params(conv_w, conv_b, d_skip, ssm_norm, dt_bias, a_log)
    ssm = _ssd(proj, small_t, gp, hp, expand, tri, jnp.asarray(_shift_matrices(), BF16), batch, seq,
               (z_blk, x_blk, b_blk, c_blk, dt_blk))

    tm_mix, tn_mix = TILE_MIX
    mixed = _mix(attn, ssm, proj, w_attn_o.astype(BF16), w_ssm_o.astype(BF16),
                 g_off // tn_mix, (g_off + d) // tn_mix, tm=tm_mix, tn=tn_mix)
    h1, f = _mix_out(mixed, x2, w_out.astype(BF16), vec(mix_norm_post), vec(ffn_norm_pre), tm=TILE_MIX_OUT_ROWS)
    dff = _ffn(f, w_gate, w_up, w_down, tm=TILE_FFN[0], tf=TILE_FFN[1])
    out = _ple(h1, dff, p_i.reshape(t, -1), vec(ffn_norm_post), vec(ple_norm_pre), vec(ple_norm_post),
               w_ple_gate.astype(BF16), w_ple.astype(BF16), tm=TILE_PLE_ROWS)
    return out.reshape(batch, seq, d)


def kernel(x, p, positions, mix_norm_pre, mix_norm_post, w_in, q_norm, w_uq, kv_norm, w_ukv, conv_w, conv_b, dt_bias, a_log, d_skip, ssm_norm, w_attn_o, w_ssm_o, w_out, ffn_norm_pre, ffn_norm_post, w_gate, w_up, w_down, ple_norm_pre, ple_norm_post, w_ple_gate, w_ple):
    invf = ROPE_THETA ** (-jnp.arange(0, QK_ROPE, 2, dtype=F32) / QK_ROPE)
    invf = invf.reshape(QK_ROPE // 2, 1)
    pos = positions.reshape(1, -1)
    h = x
    for i in range(w_in.shape[0]):
        h = _layer(h, p[i], pos, invf, mix_norm_pre[i], mix_norm_post[i], w_in[i], q_norm[i], w_uq[i],
                   kv_norm[i], w_ukv[i], conv_w[i], conv_b[i], dt_bias[i], a_log[i], d_skip[i], ssm_norm[i],
                   w_attn_o[i], w_ssm_o[i], w_out[i], ffn_norm_pre[i], ffn_norm_post[i],
                   w_gate[i], w_up[i], w_down[i], ple_norm_pre[i], ple_norm_post[i], w_ple_gate[i], w_ple[i])
    return h
```
